```python
import math, functools
import jax, jax.numpy as jnp
from jax import lax
import numpy as np

D_MODEL = 1024
BATCH = 8
SEQ = 2048
DEPTH = 4
DEC_BATCH = 128
DEC_SEQ = 8
PAST_LEN = 2048
PAGE_SIZE = 128

HEAD_DIM = 64
V_DIM = 2 * HEAD_DIM
N_HEADS = D_MODEL // (2 * HEAD_DIM)
QK_WIDTH = N_HEADS * 2 * HEAD_DIM
D_ATTN = N_HEADS * V_DIM
SM_SCALE = HEAD_DIM ** -0.5
Q_BLOCK = 128
POOL_WINDOWS = (2, 4, 8, 16)
N_POOL_GROUPS = len(POOL_WINDOWS)
D_POOL = D_MODEL // 2
POOL_GROUP_DIM = D_POOL // N_POOL_GROUPS
POOL_OUT_DIM = D_MODEL // N_POOL_GROUPS
POOL_BUF = max(POOL_WINDOWS) - 1
D_FF = -((-8 * D_MODEL) // (3 * 256)) * 256
D_IN = 2 * QK_WIDTH + D_ATTN + D_POOL + 2 * D_MODEL
ALPHA = (2.0 * DEPTH) ** 0.25
BETA = (8.0 * DEPTH) ** -0.25
EPS = 1e-5

kernel_name = "diffattn_pool_hybrid_deepnorm_adaln_step"


def layer_norm(x, g, b):
    xf = x.astype(jnp.float32)
    mu = jnp.mean(xf, axis=-1, keepdims=True)
    var = jnp.mean(jnp.square(xf - mu), axis=-1, keepdims=True)
    return ((xf - mu) * lax.rsqrt(var + EPS) * g + b).astype(x.dtype)


def alibi_slopes():
    return jnp.asarray(2.0 ** (-8.0 * np.arange(1, N_HEADS + 1) / N_HEADS), dtype=jnp.float32)


def diff_lambda(lam_l, lam_init):
    lf = lam_l.astype(jnp.float32)
    return jnp.exp(jnp.sum(lf[0] * lf[1])) - jnp.exp(jnp.sum(lf[2] * lf[3])) + lam_init


def diff_weights(s, dist, slopes, lam):
    s = s * SM_SCALE - slopes[None, :, None, None, None] * dist
    s = jnp.where(dist >= 0, s, -jnp.inf)
    p = jax.nn.softmax(s, axis=-1)
    return p[:, :, 0] - lam * p[:, :, 1]


def diff_attn_prompt(q, k, v, lam, slopes):
    B, S = q.shape[:2]
    nb = S // Q_BLOCK
    q_blocks = jnp.moveaxis(q.reshape(B, nb, Q_BLOCK, N_HEADS, 2, HEAD_DIM), 1, 0)
    k_pos = jnp.arange(S)

    def one_block(args):
        q_blk, blk = args
        q_pos = blk * Q_BLOCK + jnp.arange(Q_BLOCK)
        dist = (q_pos[:, None] - k_pos[None, :]).astype(jnp.float32)
        s = jnp.einsum('bqhmd,bkhmd->bhmqk', q_blk, k).astype(jnp.float32)
        w = diff_weights(s, dist, slopes, lam)
        return jnp.einsum('bhqk,bkhv->bqhv', w.astype(v.dtype), v)

    o = lax.map(one_block, (q_blocks, jnp.arange(nb)))
    return jnp.moveaxis(o, 0, 1).reshape(B, S, N_HEADS, V_DIM)


def diff_attn_sample(q, k, v, lam, slopes, k_past, v_past):
    T = q.shape[1]
    P = k_past.shape[1]
    q_pos = P + jnp.arange(T)
    k_pos = jnp.arange(P + T)
    dist = (q_pos[:, None] - k_pos[None, :]).astype(jnp.float32)
    s = jnp.concatenate([jnp.einsum('bqhmd,bkhmd->bhmqk', q, k_past),
                         jnp.einsum('bqhmd,bkhmd->bhmqk', q, k)], axis=-1).astype(jnp.float32)
    w = diff_weights(s, dist, slopes, lam).astype(v.dtype)
    return (jnp.einsum('bhqk,bkhv->bqhv', w[..., :P], v_past)
            + jnp.einsum('bhqk,bkhv->bqhv', w[..., P:], v))


def pool_mix(u, prefix, pos0, w_pool, pool_scale):
    B, T, _ = u.shape
    ext = jnp.concatenate([prefix, u], axis=1)
    cs = jnp.cumsum(ext.astype(jnp.float32), axis=1)
    cs = jnp.concatenate([jnp.zeros_like(cs[:, :1]), cs], axis=1)
    end = cs[:, POOL_BUF + 1:POOL_BUF + 1 + T]
    pos = pos0 + jnp.arange(T)
    means = []
    for g, w in enumerate(POOL_WINDOWS):
        sl = slice(g * POOL_GROUP_DIM, (g + 1) * POOL_GROUP_DIM)
        start = cs[:, POOL_BUF + 1 - w:POOL_BUF + 1 - w + T, sl]
        cnt = jnp.minimum(pos + 1, w).astype(jnp.float32)[None, :, None]
        means.append((end[..., sl] - start) / cnt)
    pooled = (jnp.concatenate(means, axis=-1) - u.astype(jnp.float32)).astype(u.dtype)
    pooled = pooled.reshape(B, T, N_POOL_GROUPS, POOL_GROUP_DIM)
    y = jnp.einsum('btgc,gco->btgo', pooled, w_pool).reshape(B, T, D_MODEL)
    return y * pool_scale, ext[:, -POOL_BUF:]


def decoder_layer(x, c, attend, pool_prefix, pos0, lam_init, w_in, lam, subln_g, w_pool, pool_scale,
                  w_out, w_ada, b_ada, ln1_g, ln1_b, ln2_g, ln2_b, w_gu, w_down):
    B, T, _ = x.shape
    ada = jax.nn.silu(c) @ w_ada + b_ada
    sh_m, sc_m, g_m, sh_f, sc_f, g_f = [a[:, None, :] for a in jnp.split(ada, 6, axis=-1)]
    h = x * (1 + sc_m) + sh_m
    z = h @ w_in
    q, k, v, u, gates = jnp.split(
        z, [QK_WIDTH, 2 * QK_WIDTH, 2 * QK_WIDTH + D_ATTN, 2 * QK_WIDTH + D_ATTN + D_POOL], axis=-1)
    q = q.reshape(B, T, N_HEADS, 2, HEAD_DIM)
    k = k.reshape(B, T, N_HEADS, 2, HEAD_DIM)
    v = v.reshape(B, T, N_HEADS, V_DIM)
    lam_val = diff_lambda(lam, lam_init)
    o = attend(q, k, v, lam_val).astype(jnp.float32)
    o = o * lax.rsqrt(jnp.mean(jnp.square(o), axis=-1, keepdims=True) + EPS) * subln_g * (1.0 - lam_init)
    o = o.astype(x.dtype).reshape(B, T, D_ATTN)
    pool_o, new_buf = pool_mix(u, pool_prefix, pos0, w_pool, pool_scale)
    g_a, g_p = jnp.split(gates, 2, axis=-1)
    mixed = (jax.nn.sigmoid(g_a) * o + jax.nn.sigmoid(g_p) * pool_o) @ w_out
    x = layer_norm(ALPHA * x + g_m * mixed, ln1_g, ln1_b)
    h = x * (1 + sc_f) + sh_f
    a, b = jnp.split(h @ w_gu, 2, axis=-1)
    x = layer_norm(ALPHA * x + g_f * ((jax.nn.silu(a) * b) @ w_down), ln2_g, ln2_b)
    return x, k.reshape(B, T, N_HEADS, 2 * HEAD_DIM), v, new_buf


def setup_inputs(seed: int = 0) -> dict:
    key = jax.random.key(seed)
    ks = jax.random.split(key, 24)
    f32 = jnp.float32

    def nrm(k, shape, s=1.0):
        return s * jax.random.normal(k, shape, f32)

    n_pages = PAST_LEN // PAGE_SIZE
    n_used = DEC_BATCH * n_pages
    n_pool_pages = n_used + n_used // 4

    x_prompt = nrm(ks[0], (BATCH, SEQ, D_MODEL))
    x_sample = nrm(ks[1], (DEC_BATCH, DEC_SEQ, D_MODEL))
    cache_k = nrm(ks[2], (DEPTH, n_pool_pages, PAGE_SIZE, N_HEADS, 2 * HEAD_DIM))
    cache_v = nrm(ks[3], (DEPTH, n_pool_pages, PAGE_SIZE, N_HEADS, V_DIM), BETA)
    state_pool = nrm(ks[4], (DEPTH, DEC_BATCH, POOL_BUF, D_POOL))
    page_table = jax.random.permutation(ks[5], n_pool_pages)[:n_used].reshape(DEC_BATCH, n_pages).astype(jnp.int32)
    c_prompt = nrm(ks[6], (BATCH, D_MODEL))
    c_sample = nrm(ks[7], (DEC_BATCH, D_MODEL))

    w_in = nrm(ks[8], (DEPTH, D_MODEL, D_IN), D_MODEL ** -0.5)
    w_in = w_in.at[:, :, 2 * QK_WIDTH:2 * QK_WIDTH + D_ATTN].multiply(BETA)
    lam = nrm(ks[9], (DEPTH, 4, HEAD_DIM), 0.1)
    subln_g = 1.0 + nrm(ks[10], (DEPTH, V_DIM), 0.02)
    w_pool = nrm(ks[11], (DEPTH, N_POOL_GROUPS, POOL_GROUP_DIM, POOL_OUT_DIM), POOL_GROUP_DIM ** -0.5)
    pool_scale = 1.0 + nrm(ks[12], (DEPTH, D_MODEL), 0.1)
    w_out = nrm(ks[13], (DEPTH, D_MODEL, D_MODEL), BETA * D_MODEL ** -0.5)
    w_ada = nrm(ks[14], (DEPTH, D_MODEL, 6 * D_MODEL), 0.5 * D_MODEL ** -0.5)
    b_ada = nrm(ks[15], (DEPTH, 6 * D_MODEL), 0.01)
    ln1_g = 1.0 + nrm(ks[16], (DEPTH, D_MODEL), 0.02)
    ln1_b = nrm(ks[17], (DEPTH, D_MODEL), 0.01)
    ln2_g = 1.0 + nrm(ks[18], (DEPTH, D_MODEL), 0.02)
    ln2_b = nrm(ks[19], (DEPTH, D_MODEL), 0.01)
    w_gu = nrm(ks[20], (DEPTH, D_MODEL, 2 * D_FF), D_MODEL ** -0.5)
    w_down = nrm(ks[21], (DEPTH, D_FF, D_MODEL), BETA * D_FF ** -0.5)
    return {"x_prompt": x_prompt, "x_sample": x_sample, "cache_k": cache_k, "cache_v": cache_v,
            "state_pool": state_pool, "page_table": page_table, "c_prompt": c_prompt, "c_sample": c_sample,
            "w_in": w_in, "lam": lam, "subln_g": subln_g, "w_pool": w_pool, "pool_scale": pool_scale,
            "w_out": w_out, "w_ada": w_ada, "b_ada": b_ada, "ln1_g": ln1_g, "ln1_b": ln1_b,
            "ln2_g": ln2_g, "ln2_b": ln2_b, "w_gu": w_gu, "w_down": w_down}


def reference(x_prompt, x_sample, cache_k, cache_v, state_pool, page_table, c_prompt, c_sample,
              w_in, lam, subln_g, w_pool, pool_scale, w_out, w_ada, b_ada, ln1_g, ln1_b, ln2_g, ln2_b,
              w_gu, w_down):
    slopes = alibi_slopes()
    n_dec, n_pages = page_table.shape
    past_len = n_pages * PAGE_SIZE
    xp, xs = x_prompt, x_sample
    zero_prefix = jnp.zeros((xp.shape[0], POOL_BUF, D_POOL), xp.dtype)
    attend_prompt = functools.partial(diff_attn_prompt, slopes=slopes)
    kp, vp, bp, ks_, vs_, bs_ = [], [], [], [], [], []
    for l in range(DEPTH):
        lam_init = 0.8 - 0.6 * math.exp(-0.3 * l)
        wl = (w_in[l], lam[l], subln_g[l], w_pool[l], pool_scale[l], w_out[l], w_ada[l], b_ada[l],
              ln1_g[l], ln1_b[l], ln2_g[l], ln2_b[l], w_gu[l], w_down[l])
        xp, k_l, v_l, buf_l = decoder_layer(xp, c_prompt, attend_prompt, zero_prefix, 0, lam_init, *wl)
        kp.append(k_l); vp.append(v_l); bp.append(buf_l)
        k_past = cache_k[l, page_table].reshape(n_dec, past_len, N_HEADS, 2, HEAD_DIM)
        v_past = cache_v[l, page_table].reshape(n_dec, past_len, N_HEADS, V_DIM)
        attend_sample = functools.partial(diff_attn_sample, slopes=slopes, k_past=k_past, v_past=v_past)
        xs, k_s, v_s, buf_s = decoder_layer(xs, c_sample, attend_sample, state_pool[l], past_len, lam_init, *wl)
        ks_.append(k_s); vs_.append(v_s); bs_.append(buf_s)
    return (xp, xs, jnp.stack(kp), jnp.stack(vp), jnp.stack(bp), jnp.stack(ks_), jnp.stack(vs_), jnp.stack(bs_))
```

```python
import functools
import math

import numpy as np
import jax
import jax.numpy as jnp
from jax import lax
from jax.experimental import pallas as pl
from jax.experimental.pallas import tpu as pltpu

D_MODEL = 1024
DEPTH = 4
PAGE_SIZE = 128
HEAD_DIM = 64
V_DIM = 2 * HEAD_DIM
N_HEADS = D_MODEL // V_DIM
QK_WIDTH = N_HEADS * 2 * HEAD_DIM
SM_SCALE = HEAD_DIM ** -0.5
POOL_WINDOWS = (2, 4, 8, 16)
N_POOL_GROUPS = len(POOL_WINDOWS)
D_POOL = D_MODEL // 2
POOL_GROUP_DIM = D_POOL // N_POOL_GROUPS
POOL_OUT_DIM = D_MODEL // N_POOL_GROUPS
POOL_BUF = max(POOL_WINDOWS) - 1
HALO = POOL_BUF + 1
D_FF = -((-8 * D_MODEL) // (3 * 256)) * 256
D_IN = 2 * QK_WIDTH + D_MODEL + D_POOL + 2 * D_MODEL
ALPHA = (2.0 * DEPTH) ** 0.25
EPS = 1e-5
MASKED = 1e30

V7X_VMEM_LIMIT_BYTES = 56 * 1024 * 1024
IN_TILE_N = 512
FF_TILE = 256
ATTN_TILE = 512
PAGES_PER_STEP = 4

F32 = jnp.float32
BF16 = jnp.bfloat16


def _params(semantics):
    return pltpu.CompilerParams(dimension_semantics=semantics, vmem_limit_bytes=V7X_VMEM_LIMIT_BYTES)


def _layer_norm(y, g, b):
    mu = jnp.mean(y, axis=-1, keepdims=True)
    yc = y - mu
    var = jnp.mean(yc * yc, axis=-1, keepdims=True)
    return yc * lax.rsqrt(var + EPS) * g + b


def _ada_kernel(c_ref, w_ref, b_ref, o_ref):
    c = c_ref[...]
    s = (c * jax.nn.sigmoid(c)).astype(BF16)
    o_ref[...] = jnp.dot(s, w_ref[...].astype(BF16), preferred_element_type=F32) + b_ref[...]


def _ada_call(c_all, w_ada, b_ada):
    n = c_all.shape[0]
    tn = 1024
    return pl.pallas_call(
        _ada_kernel,
        grid=(DEPTH, 6 * D_MODEL // tn),
        in_specs=[pl.BlockSpec((n, D_MODEL), lambda l, j: (0, 0)),
                  pl.BlockSpec((None, D_MODEL, tn), lambda l, j: (l, 0, j)),
                  pl.BlockSpec((None, 1, tn), lambda l, j: (l, 0, j))],
        out_specs=pl.BlockSpec((None, n, tn), lambda l, j: (l, 0, j)),
        out_shape=jax.ShapeDtypeStruct((DEPTH, n, 6 * D_MODEL), F32),
        compiler_params=_params(("arbitrary", "arbitrary")),
        name="ada",
    )(c_all, w_ada, b_ada.reshape(DEPTH, 1, 6 * D_MODEL))


_Q0, _K0, _V0, _U0, _G0, _NIN = 0, 2, 4, 6, 7, 11


def _inproj_kernel(x_ref, sc_ref, sh_ref, w_ref, q_ref, k_ref, v_ref, kb_ref, vb_ref, u_ref, g_ref, h_scr):
    j = pl.program_id(2)

    @pl.when(j == 0)
    def _():
        h = x_ref[...] * (1.0 + sc_ref[...]) + sh_ref[...]
        h_scr[...] = h.reshape(h_scr.shape).astype(BF16)

    z = jnp.dot(h_scr[...], w_ref[...].astype(BF16), preferred_element_type=F32)

    @pl.when(j < _K0)
    def _():
        q_ref[...] = (z * SM_SCALE).astype(BF16)

    @pl.when((j >= _K0) & (j < _V0))
    def _():
        k_ref[...] = z
        kb_ref[...] = z.astype(BF16)

    @pl.when((j >= _V0) & (j < _U0))
    def _():
        v_ref[...] = z
        vb_ref[...] = z.astype(BF16)

    @pl.when(j == _U0)
    def _():
        u_ref[...] = z

    @pl.when(j >= _G0)
    def _():
        g_ref[...] = z


def _inproj_call(x3, mod3, w_in, layer, a_blk, r_blk):
    na, r, _ = x3.shape
    g0, g1 = na // a_blk, r // r_blk
    tm = a_blk * r_blk
    m = na * r
    tn = IN_TILE_N

    def row(i0, i1, j):
        return i0 * g1 + i1

    def seg(first, count):
        return lambda i0, i1, j: (row(i0, i1, j), jnp.clip(j - first, 0, count - 1))

    mod_base = layer * g0
    out_shapes = (jax.ShapeDtypeStruct((m, QK_WIDTH), BF16),
                  jax.ShapeDtypeStruct((m, QK_WIDTH), F32),
                  jax.ShapeDtypeStruct((m, D_MODEL), F32),
                  jax.ShapeDtypeStruct((m, QK_WIDTH), BF16),
                  jax.ShapeDtypeStruct((m, D_MODEL), BF16),
                  jax.ShapeDtypeStruct((m, D_POOL), F32),
                  jax.ShapeDtypeStruct((m, 2 * D_MODEL), F32))
    return pl.pallas_call(
        _inproj_kernel,
        grid=(g0, g1, _NIN),
        in_specs=[pl.BlockSpec((a_blk, r_blk, D_MODEL), lambda i0, i1, j: (i0, i1, 0)),
                  pl.BlockSpec((a_blk, 1, D_MODEL), lambda i0, i1, j: (mod_base + i0, 0, 1)),
                  pl.BlockSpec((a_blk, 1, D_MODEL), lambda i0, i1, j: (mod_base + i0, 0, 0)),
                  pl.BlockSpec((None, D_MODEL, tn), lambda i0, i1, j: (layer, 0, j))],
        out_specs=[pl.BlockSpec((tm, tn), seg(_Q0, 2)),
                   pl.BlockSpec((tm, tn), seg(_K0, 2)),
                   pl.BlockSpec((tm, tn), seg(_V0, 2)),
                   pl.BlockSpec((tm, tn), seg(_K0, 2)),
                   pl.BlockSpec((tm, tn), seg(_V0, 2)),
                   pl.BlockSpec((tm, tn), seg(_U0, 1)),
                   pl.BlockSpec((tm, tn), seg(_G0, 4))],
        out_shape=out_shapes,
        scratch_shapes=[pltpu.VMEM((tm, D_MODEL), BF16)],
        compiler_params=_params(("arbitrary", "arbitrary", "arbitrary")),
        name="inproj",
    )(x3, mod3, mod3, w_in)


def _split_maps(q):
    lane = lax.broadcasted_iota(jnp.int32, q.shape, 1)
    qf = q.astype(F32)
    halves = [jnp.where(lane < HEAD_DIM, qf, 0.0), jnp.where(lane >= HEAD_DIM, qf, 0.0)]
    return jnp.concatenate(halves, axis=0).astype(BF16)


def _online_softmax_step(s, off, v, m_scr, l_scr, acc_scr):
    n, t = s.shape
    m_prev = m_scr[...]
    m_next = jnp.maximum(m_prev, jnp.max(s, axis=1, keepdims=True) - off)
    alpha = jnp.exp(m_prev - m_next)
    shift = m_next + off
    if t >= 128:
        p = jnp.exp(s - pltpu.repeat(shift, t // 128, axis=1))
    else:
        p = jnp.exp(s - shift[:, :t])
    l_scr[...] = alpha * l_scr[...] + jnp.sum(p, axis=1, keepdims=True)
    acc_scr[...] = alpha * acc_scr[...] + jnp.dot(p.astype(BF16), v, preferred_element_type=F32)
    m_scr[...] = m_next


def _diff_finalize(lam_ref, g_ref, l_scr, acc_scr, lam_init):
    n = acc_scr.shape[0] // 2
    o = acc_scr[...] / l_scr[...]
    lam = lam_ref[...]
    lam_val = (jnp.exp(jnp.sum(lam[0:1] * lam[1:2], axis=1, keepdims=True))
               - jnp.exp(jnp.sum(lam[2:3] * lam[3:4], axis=1, keepdims=True)) + lam_init)
    od = o[:n] - lam_val * o[n:]
    od = od * lax.rsqrt(jnp.mean(od * od, axis=1, keepdims=True) + EPS)
    return od * g_ref[...] * (1.0 - lam_init)


def _init_softmax_state(m_scr, l_scr, acc_scr):
    m_scr[...] = jnp.full(m_scr.shape, -MASKED, F32)
    l_scr[...] = jnp.zeros(l_scr.shape, F32)
    acc_scr[...] = jnp.zeros(acc_scr.shape, F32)


def _alibi_slopes():
    return 2.0 ** (-8.0 * np.arange(1, N_HEADS + 1) / N_HEADS)


def _attn_prompt_kernel(slope_ref, q_ref, k_ref, v_ref, rel_ref, diag_ref, lam_ref, g_ref, o_ref,
                        q2_scr, m_scr, l_scr, acc_scr, *, lam_init):
    h, qi, ki = pl.program_id(1), pl.program_id(2), pl.program_id(3)
    t = q_ref.shape[0]

    @pl.when(ki == 0)
    def _():
        q2_scr[...] = _split_maps(q_ref[...])
        _init_softmax_state(m_scr, l_scr, acc_scr)

    def step(bias_ref):
        s = lax.dot_general(q2_scr[...], k_ref[...], (((1,), (1,)), ((), ())), preferred_element_type=F32)
        s = (s.reshape(2, t, t) - bias_ref[...][None]).reshape(2 * t, t)
        off = slope_ref[h] * ((qi - ki) * t).astype(F32)
        _online_softmax_step(s, off, v_ref[...], m_scr, l_scr, acc_scr)

    @pl.when(ki < qi)
    def _():
        step(rel_ref)

    @pl.when(ki == qi)
    def _():
        step(diag_ref)
        o_ref[...] = _diff_finalize(lam_ref, g_ref, l_scr, acc_scr, lam_init)


def _prompt_bias_tables(t):
    r = lax.broadcasted_iota(jnp.int32, (t, t), 0)
    c = lax.broadcasted_iota(jnp.int32, (t, t), 1)
    rel = (r - c).astype(F32)[None] * jnp.asarray(_alibi_slopes(), F32)[:, None, None]
    diag = jnp.where((r >= c)[None], rel, MASKED)
    return rel, diag


def _attn_prompt_call(q, kb, vb, rel, diag, slopes, lam, subln_g, layer, lam_init, batch, seq):
    t = ATTN_TILE
    nq = seq // t

    def q_map(b, h, qi, ki):
        return (b * nq + qi, h)

    def kv_map(b, h, qi, ki):
        return (b * nq + jnp.minimum(ki, qi), h)

    return pl.pallas_call(
        functools.partial(_attn_prompt_kernel, lam_init=lam_init),
        grid=(batch, N_HEADS, nq, nq),
        in_specs=[pl.BlockSpec(memory_space=pltpu.SMEM),
                  pl.BlockSpec((t, V_DIM), q_map),
                  pl.BlockSpec((t, V_DIM), kv_map),
                  pl.BlockSpec((t, V_DIM), kv_map),
                  pl.BlockSpec((None, t, t), lambda b, h, qi, ki: (h, 0, 0)),
                  pl.BlockSpec((None, t, t), lambda b, h, qi, ki: (h, 0, 0)),
                  pl.BlockSpec((None, 4, HEAD_DIM), lambda b, h, qi, ki: (layer, 0, 0)),
                  pl.BlockSpec((None, 1, V_DIM), lambda b, h, qi, ki: (layer, 0, 0))],
        out_specs=pl.BlockSpec((t, V_DIM), q_map),
        out_shape=jax.ShapeDtypeStruct((batch * seq, D_MODEL), F32),
        scratch_shapes=[pltpu.VMEM((2 * t, V_DIM), BF16),
                        pltpu.VMEM((2 * t, V_DIM), F32),
                        pltpu.VMEM((2 * t, V_DIM), F32),
                        pltpu.VMEM((2 * t, V_DIM), F32)],
        compiler_params=_params(("arbitrary",) * 4),
        name="attn_prompt",
    )(slopes, q, kb, vb, rel, diag, lam, subln_g.reshape(DEPTH, 1, V_DIM))


def _attn_sample_kernel(pt_ref, q_ref, kn_ref, vn_ref, *rest, lam_init, past_len, pages_per_step):
    del pt_ref
    pp = pages_per_step
    kp_refs, vp_refs = rest[:pp], rest[pp:2 * pp]
    bias_ref, biasn_ref, rs_ref, lam_ref, g_ref, o_ref, q2_scr, m_scr, l_scr, acc_scr = rest[2 * pp:]
    j = pl.program_id(1)

    @pl.when(j == 0)
    def _():
        q2_scr[...] = _split_maps(q_ref[...])
        _init_softmax_state(m_scr, l_scr, acc_scr)

    for i in range(pp):
        page = j * pp + i
        off = rs_ref[...] * (past_len - page * PAGE_SIZE).astype(F32)
        k = kp_refs[i][...].astype(BF16)
        s = lax.dot_general(q2_scr[...], k, (((1,), (1,)), ((), ())), preferred_element_type=F32)
        _online_softmax_step(s - bias_ref[...], off, vp_refs[i][...].astype(BF16), m_scr, l_scr, acc_scr)

    @pl.when(j == pl.num_programs(1) - 1)
    def _():
        s = lax.dot_general(q2_scr[...], kn_ref[...], (((1,), (1,)), ((), ())), preferred_element_type=F32)
        _online_softmax_step(s - biasn_ref[...], 0.0, vn_ref[...], m_scr, l_scr, acc_scr)
        o_ref[...] = _diff_finalize(lam_ref, g_ref, l_scr, acc_scr, lam_init)


def _sample_bias_tables(dec_seq):
    rows = 2 * dec_seq * N_HEADS
    r = np.arange(rows)
    r_head, r_tok = r % N_HEADS, (r // N_HEADS) % dec_seq
    slope = _alibi_slopes()[r_head]
    c = np.arange(PAGE_SIZE * N_HEADS)
    c_head, c_pos = c % N_HEADS, c // N_HEADS
    same = r_head[:, None] == c_head[None, :]
    page = np.where(same, slope[:, None] * (r_tok[:, None] - c_pos[None, :]), MASKED)
    cn = np.arange(dec_seq * N_HEADS)
    cn_head, cn_tok = cn % N_HEADS, cn // N_HEADS
    ok = (r_head[:, None] == cn_head[None, :]) & (cn_tok[None, :] <= r_tok[:, None])
    new = np.where(ok, slope[:, None] * (r_tok[:, None] - cn_tok[None, :]), MASKED)
    row_slope = np.broadcast_to(slope[:, None], (rows, V_DIM))
    return (jnp.asarray(page, F32), jnp.asarray(new, F32), jnp.asarray(row_slope, F32))


def _attn_sample_call(page_table, q, kb, vb, cache_k4, cache_v4, tables, lam, subln_g, layer, lam_init, dec_seq):
    n_dec, n_pages = page_table.shape
    pp = PAGES_PER_STEP
    rows_new = dec_seq * N_HEADS
    rows_q = 2 * rows_new
    page_rows = PAGE_SIZE * N_HEADS
    bias_page, bias_new, row_slope = tables
    q3 = q.reshape(n_dec, rows_new, V_DIM)
    kn3 = kb.reshape(n_dec, rows_new, V_DIM)
    vn3 = vb.reshape(n_dec, rows_new, V_DIM)

    def seq_map(b, j, pt):
        return (b, 0, 0)

    def page_map(i):
        return lambda b, j, pt: (layer, pt[b, j * pp + i], 0, 0)

    def const2(b, j, pt):
        return (0, 0)

    page_spec = [pl.BlockSpec((None, None, page_rows, V_DIM), page_map(i)) for i in range(pp)]
    grid_spec = pltpu.PrefetchScalarGridSpec(
        num_scalar_prefetch=1,
        grid=(n_dec, n_pages // pp),
        in_specs=[pl.BlockSpec((None, rows_new, V_DIM), seq_map),
                  pl.BlockSpec((None, rows_new, V_DIM), seq_map),
                  pl.BlockSpec((None, rows_new, V_DIM), seq_map),
                  *page_spec, *page_spec,
                  pl.BlockSpec((rows_q, page_rows), const2),
                  pl.BlockSpec((rows_q, rows_new), const2),
                  pl.BlockSpec((rows_q, V_DIM), const2),
                  pl.BlockSpec((None, 4, HEAD_DIM), lambda b, j, pt: (layer, 0, 0)),
                  pl.BlockSpec((None, 1, V_DIM), lambda b, j, pt: (layer, 0, 0))],
        out_specs=pl.BlockSpec((None, rows_new, V_DIM), seq_map),
        scratch_shapes=[pltpu.VMEM((rows_q, V_DIM), BF16),
                        pltpu.VMEM((rows_q, V_DIM), F32),
                        pltpu.VMEM((rows_q, V_DIM), F32),
                        pltpu.VMEM((rows_q, V_DIM), F32)])
    o3 = pl.pallas_call(
        functools.partial(_attn_sample_kernel, lam_init=lam_init, past_len=n_pages * PAGE_SIZE, pages_per_step=pp),
        grid_spec=grid_spec,
        out_shape=jax.ShapeDtypeStruct((n_dec, rows_new, V_DIM), F32),
        compiler_params=_params(("arbitrary", "arbitrary")),
        name="attn_sample",
    )(page_table, q3, kn3, vn3, *([cache_k4] * pp), *([cache_v4] * pp),
      bias_page, bias_new, row_slope, lam, subln_g.reshape(DEPTH, 1, V_DIM))
    return o3.reshape(n_dec * dec_seq, D_MODEL)


def _merge_kernel(x_ref, o_ref, u_ref, halo_ref, g_ref, gm_ref, wp_ref, ps_ref, wo_ref, lg_ref, lb_ref,
                  y_ref, ext_scr, wo_scr, wp_scr, *, pos0, zero_first_halo):
    i0, i1 = pl.program_id(0), pl.program_id(1)
    a, r, _ = u_ref.shape
    tm = a * r

    @pl.when((i0 == 0) & (i1 == 0))
    def _():
        wo_scr[...] = wo_ref[...].astype(BF16)
        wp_scr[...] = wp_ref[...].astype(BF16)

    halo = halo_ref[...]
    if zero_first_halo:
        halo = jnp.where(i1 == 0, jnp.zeros_like(halo), halo)
    ext_scr[:, 0:HALO, :] = halo
    ext_scr[:, HALO:HALO + r, :] = u_ref[...]

    pos = pos0 + i1 * r + lax.broadcasted_iota(jnp.int32, (a, r, POOL_GROUP_DIM), 1)
    pool_parts = []
    for g, w in enumerate(POOL_WINDOWS):
        lanes = slice(g * POOL_GROUP_DIM, (g + 1) * POOL_GROUP_DIM)
        tot = ext_scr[:, HALO:HALO + r, lanes]
        for k in range(1, w):
            tot = tot + ext_scr[:, HALO - k:HALO - k + r, lanes]
        cnt = jnp.minimum(pos + 1, w).astype(F32)
        pooled = tot / cnt - ext_scr[:, HALO:HALO + r, lanes]
        pool_parts.append(jnp.dot(pooled.reshape(tm, POOL_GROUP_DIM).astype(BF16), wp_scr[g],
                                  preferred_element_type=F32))
    pool_o = jnp.concatenate(pool_parts, axis=1) * ps_ref[...]

    gates = g_ref[...]
    mixed = (jax.nn.sigmoid(gates[:, :D_MODEL]) * o_ref[...]
             + jax.nn.sigmoid(gates[:, D_MODEL:]) * pool_o)
    out = jnp.dot(mixed.astype(BF16), wo_scr[...], preferred_element_type=F32)
    y = ALPHA * x_ref[...] + gm_ref[...] * out.reshape(a, r, D_MODEL)
    y_ref[...] = _layer_norm(y, lg_ref[...], lb_ref[...])


def _merge_call(x3, o, u, halo_src, halo_map, gates, mod3, w_pool, pool_scale, w_out, ln_g, ln_b,
                layer, a_blk, r_blk, pos0, zero_first_halo):
    na, r, _ = x3.shape
    g0, g1 = na // a_blk, r // r_blk
    tm = a_blk * r_blk
    mod_base = layer * g0
    u3 = u.reshape(na, r, D_POOL)

    def row(i0, i1):
        return (i0 * g1 + i1, 0)

    def tok(i0, i1):
        return (i0, i1, 0)

    def vec(i0, i1):
        return (layer, 0, 0)

    return pl.pallas_call(
        functools.partial(_merge_kernel, pos0=pos0, zero_first_halo=zero_first_halo),
        grid=(g0, g1),
        in_specs=[pl.BlockSpec((a_blk, r_blk, D_MODEL), tok),
                  pl.BlockSpec((tm, D_MODEL), row),
                  pl.BlockSpec((a_blk, r_blk, D_POOL), tok),
                  pl.BlockSpec((a_blk, HALO, D_POOL), halo_map),
                  pl.BlockSpec((tm, 2 * D_MODEL), row),
                  pl.BlockSpec((a_blk, 1, D_MODEL), lambda i0, i1: (mod_base + i0, 0, 2)),
                  pl.BlockSpec((None, N_POOL_GROUPS, POOL_GROUP_DIM, POOL_OUT_DIM), lambda i0, i1: (layer, 0, 0, 0)),
                  pl.BlockSpec((None, 1, D_MODEL), vec),
                  pl.BlockSpec((None, D_MODEL, D_MODEL), vec),
                  pl.BlockSpec((None, 1, D_MODEL), vec),
                  pl.BlockSpec((None, 1, D_MODEL), vec)],
        out_specs=pl.BlockSpec((a_blk, r_blk, D_MODEL), tok),
        out_shape=jax.ShapeDtypeStruct(x3.shape, F32),
        scratch_shapes=[pltpu.VMEM((a_blk, HALO + r_blk, D_POOL), F32),
                        pltpu.VMEM((D_MODEL, D_MODEL), BF16),
                        pltpu.VMEM((N_POOL_GROUPS, POOL_GROUP_DIM, POOL_OUT_DIM), BF16)],
        compiler_params=_params(("arbitrary", "arbitrary")),
        name="merge",
    )(x3, o, u3, halo_src, gates, mod3, w_pool, pool_scale.reshape(DEPTH, 1, D_MODEL), w_out,
      ln_g.reshape(DEPTH, 1, D_MODEL), ln_b.reshape(DEPTH, 1, D_MODEL))


def _ffn_kernel(x_ref, sc_ref, sh_ref, gf_ref, wa_ref, wb_ref, wd_ref, lg_ref, lb_ref, y_ref, h_scr, acc_scr):
    f = pl.program_id(2)

    @pl.when(f == 0)
    def _():
        h = x_ref[...] * (1.0 + sc_ref[...]) + sh_ref[...]
        h_scr[...] = h.reshape(h_scr.shape).astype(BF16)
        acc_scr[...] = jnp.zeros(acc_scr.shape, F32)

    h = h_scr[...]
    a = jnp.dot(h, wa_ref[...].astype(BF16), preferred_element_type=F32)
    b = jnp.dot(h, wb_ref[...].astype(BF16), preferred_element_type=F32)
    act = (a * jax.nn.sigmoid(a) * b).astype(BF16)
    acc_scr[...] += jnp.dot(act, wd_ref[...].astype(BF16), preferred_element_type=F32)

    @pl.when(f == pl.num_programs(2) - 1)
    def _():
        y = ALPHA * x_ref[...] + gf_ref[...] * acc_scr[...].reshape(x_ref.shape)
        y_ref[...] = _layer_norm(y, lg_ref[...], lb_ref[...])


def _ffn_call(x3, mod3, w_gu, w_down, ln_g, ln_b, layer, a_blk, r_blk):
    na, r, _ = x3.shape
    g0, g1 = na // a_blk, r // r_blk
    tm = a_blk * r_blk
    nf = D_FF // FF_TILE
    mod_base = layer * g0

    def tok(i0, i1, f):
        return (i0, i1, 0)

    def vec(i0, i1, f):
        return (layer, 0, 0)

    def mod(which):
        return lambda i0, i1, f: (mod_base + i0, 0, which)

    return pl.pallas_call(
        _ffn_kernel,
        grid=(g0, g1, nf),
        in_specs=[pl.BlockSpec((a_blk, r_blk, D_MODEL), tok),
                  pl.BlockSpec((a_blk, 1, D_MODEL), mod(4)),
                  pl.BlockSpec((a_blk, 1, D_MODEL), mod(3)),
                  pl.BlockSpec((a_blk, 1, D_MODEL), mod(5)),
                  pl.BlockSpec((None, D_MODEL, FF_TILE), lambda i0, i1, f: (layer, 0, f)),
                  pl.BlockSpec((None, D_MODEL, FF_TILE), lambda i0, i1, f: (layer, 0, nf + f)),
                  pl.BlockSpec((None, FF_TILE, D_MODEL), lambda i0, i1, f: (layer, f, 0)),
                  pl.BlockSpec((None, 1, D_MODEL), vec),
                  pl.BlockSpec((None, 1, D_MODEL), vec)],
        out_specs=pl.BlockSpec((a_blk, r_blk, D_MODEL), tok),
        out_shape=jax.ShapeDtypeStruct(x3.shape, F32),
        scratch_shapes=[pltpu.VMEM((tm, D_MODEL), BF16), pltpu.VMEM((tm, D_MODEL), F32)],
        compiler_params=_params(("arbitrary", "arbitrary", "arbitrary")),
        name="ffn",
    )(x3, mod3, mod3, mod3, w_gu, w_gu, w_down, ln_g.reshape(DEPTH, 1, D_MODEL), ln_b.reshape(DEPTH, 1, D_MODEL))


def kernel(x_prompt, x_sample, cache_k, cache_v, state_pool, page_table, c_prompt, c_sample, w_in, lam, subln_g,
           w_pool, pool_scale, w_out, w_ada, b_ada, ln1_g, ln1_b, ln2_g, ln2_b, w_gu, w_down):
    batch, seq, _ = x_prompt.shape
    n_dec, dec_seq, _ = x_sample.shape
    n_pages = page_table.shape[1]
    assert seq % ATTN_TILE == 0 and n_pages % PAGES_PER_STEP == 0

    ada = _ada_call(jnp.concatenate([c_prompt, c_sample], axis=0), w_ada, b_ada)
    mod_p = ada[:, :batch].reshape(DEPTH * batch, 1, 6 * D_MODEL)
    mod_s = ada[:, batch:].reshape(DEPTH * n_dec, 1, 6 * D_MODEL)

    n_pool_pages = cache_k.shape[1]
    cache_k4 = cache_k.reshape(DEPTH, n_pool_pages, PAGE_SIZE * N_HEADS, V_DIM)
    cache_v4 = cache_v.reshape(DEPTH, n_pool_pages, PAGE_SIZE * N_HEADS, V_DIM)
    prefix = jnp.pad(state_pool, ((0, 0), (0, 0), (1, 0), (0, 0))).reshape(DEPTH * n_dec, HALO, D_POOL)

    rel, diag = _prompt_bias_tables(ATTN_TILE)
    slopes = jnp.asarray(_alibi_slopes(), F32)
    sample_tables = _sample_bias_tables(dec_seq)

    pa_in, pr_in = 1, 1024
    pa_mg, pr_mg = 1, 512
    pa_ff, pr_ff = 1, 1024
    sa_in, sa_mg, sa_ff, sr = n_dec // 2, n_dec // 4, n_dec // 2, dec_seq

    xp, xs = x_prompt, x_sample
    kp, vp, bp, ks, vs, bs = [], [], [], [], [], []
    for layer in range(DEPTH):
        lam_init = 0.8 - 0.6 * math.exp(-0.3 * layer)

        q, k, v, kb, vb, u, gates = _inproj_call(xp, mod_p, w_in, layer, pa_in, pr_in)
        o = _attn_prompt_call(q, kb, vb, rel, diag, slopes, lam, subln_g, layer, lam_init, batch, seq)
        halo_blocks = pr_mg // HALO
        u3 = u.reshape(batch, seq, D_POOL)
        xp = _merge_call(xp, o, u, u3, lambda i0, i1: (i0, jnp.maximum(i1 * halo_blocks - 1, 0), 0),
                         gates, mod_p, w_pool, pool_scale, w_out, ln1_g, ln1_b, layer, pa_mg, pr_mg, 0, True)
        xp = _ffn_call(xp, mod_p, w_gu, w_down, ln2_g, ln2_b, layer, pa_ff, pr_ff)
        kp.append(k.reshape(batch, seq, N_HEADS, V_DIM))
        vp.append(v.reshape(batch, seq, N_HEADS, V_DIM))
        bp.append(u3[:, seq - POOL_BUF:])

        q, k, v, kb, vb, u, gates = _inproj_call(xs, mod_s, w_in, layer, sa_in, sr)
        o = _attn_sample_call(page_table, q, kb, vb, cache_k4, cache_v4, sample_tables, lam, subln_g,
                              layer, lam_init, dec_seq)
        xs = _merge_call(xs, o, u, prefix, lambda i0, i1: (layer * (n_dec // sa_mg) + i0, 0, 0),
                         gates, mod_s, w_pool, pool_scale, w_out, ln1_g, ln1_b, layer, sa_mg, sr,
                         n_pages * PAGE_SIZE, False)
        xs = _ffn_call(xs, mod_s, w_gu, w_down, ln2_g, ln2_b, layer, sa_ff, sr)
        ks.append(k.reshape(n_dec, dec_seq, N_HEADS, V_DIM))
        vs.append(v.reshape(n_dec, dec_seq, N_HEADS, V_DIM))
        us = u.reshape(n_dec, dec_seq, D_POOL)
        bs.append(jnp.concatenate([state_pool[layer], us], axis=1)[:, -POOL_BUF:])

    return (xp, xs, jnp.stack(kp), jnp.stack(vp), jnp.stack(bp), jnp.stack(ks), jnp.stack(vs), jnp.stack(bs))
```

```python
import functools
import math

import numpy as np
import jax
import jax.numpy as jnp
from jax import lax
from jax.experimental import pallas as pl
from jax.experimental.pallas import tpu as pltpu

D_MODEL = 1024
DEPTH = 4
PAGE_SIZE = 128
HEAD_DIM = 64
V_DIM = 2 * HEAD_DIM
N_HEADS = D_MODEL // V_DIM
QK_WIDTH = N_HEADS * 2 * HEAD_DIM
SM_SCALE = HEAD_DIM ** -0.5
LOG2E = math.log2(math.e)
POOL_WINDOWS = (2, 4, 8, 16)
N_POOL_GROUPS = len(POOL_WINDOWS)
D_POOL = D_MODEL // 2
POOL_GROUP_DIM = D_POOL // N_POOL_GROUPS
POOL_OUT_DIM = D_MODEL // N_POOL_GROUPS
POOL_BUF = max(POOL_WINDOWS) - 1
HALO = POOL_BUF + 1
D_FF = -((-8 * D_MODEL) // (3 * 256)) * 256
D_IN = 2 * QK_WIDTH + D_MODEL + D_POOL + 2 * D_MODEL
ALPHA = (2.0 * DEPTH) ** 0.25
EPS = 1e-5
MASKED = 1e30

V7X_VMEM_LIMIT_BYTES = 56 * 1024 * 1024
IN_CHUNK = 512
FF_CHUNKS = ((0, 1280), (1280, D_FF))
ATTN_TILE = 512
PAGES_PER_STEP = 8

F32 = jnp.float32
BF16 = jnp.bfloat16


def _params(semantics):
    return pltpu.CompilerParams(dimension_semantics=semantics, vmem_limit_bytes=V7X_VMEM_LIMIT_BYTES)


def _layer_norm(y, g, b):
    mu = jnp.mean(y, axis=-1, keepdims=True)
    yc = y - mu
    var = jnp.mean(yc * yc, axis=-1, keepdims=True)
    return yc * lax.rsqrt(var + EPS) * g + b


def _mod_spec(mod, layer, tm, tiles_per_row_group, which):
    if mod.ndim == 3:
        base = layer * (mod.shape[0] // DEPTH)
        return pl.BlockSpec((None, 1, D_MODEL), lambda i, *_: (base + i // tiles_per_row_group, 0, which))
    base = layer * (mod.shape[0] // DEPTH // tm)
    return pl.BlockSpec((tm, D_MODEL), lambda i, *_: (base + i, which))


def _ada_kernel(cp_ref, cs_ref, w_ref, b_ref, op_ref, os_ref):
    w = w_ref[...]
    for c_ref, o_ref in ((cp_ref, op_ref), (cs_ref, os_ref)):
        c = c_ref[...]
        s = (c * jax.nn.sigmoid(c)).astype(BF16)
        o_ref[...] = jnp.dot(s, w, preferred_element_type=F32) + b_ref[...]


def _ada_call(c_p, c_s, w_ada, b_ada):
    n_p, n_s = c_p.shape[0], c_s.shape[0]
    tn = 1024
    return pl.pallas_call(
        _ada_kernel,
        grid=(DEPTH, 6 * D_MODEL // tn),
        in_specs=[pl.BlockSpec((n_p, D_MODEL), lambda l, j: (0, 0)),
                  pl.BlockSpec((n_s, D_MODEL), lambda l, j: (0, 0)),
                  pl.BlockSpec((None, D_MODEL, tn), lambda l, j: (l, 0, j)),
                  pl.BlockSpec((None, 1, tn), lambda l, j: (l, 0, j))],
        out_specs=[pl.BlockSpec((None, n_p, tn), lambda l, j: (l, 0, j)),
                   pl.BlockSpec((None, n_s, tn), lambda l, j: (l, 0, j))],
        out_shape=(jax.ShapeDtypeStruct((DEPTH, n_p, 6 * D_MODEL), F32),
                   jax.ShapeDtypeStruct((DEPTH, n_s, 6 * D_MODEL), F32)),
        compiler_params=_params(("arbitrary", "arbitrary")),
        name="ada",
    )(c_p, c_s, w_ada, b_ada.reshape(DEPTH, 1, 6 * D_MODEL))


_K_COL, _V_COL, _U_COL, _G_COL = QK_WIDTH, 2 * QK_WIDTH, 2 * QK_WIDTH + D_MODEL, 2 * QK_WIDTH + D_MODEL + D_POOL


def _inproj_kernel(x_ref, sc_ref, sh_ref, w_ref, q_ref, k_ref, v_ref, kb_ref, vb_ref, u_ref, g_ref):
    h = (x_ref[...] * (1.0 + sc_ref[...]) + sh_ref[...]).astype(BF16)

    def mm(col):
        return jnp.dot(h, w_ref[:, col:col + IN_CHUNK], preferred_element_type=F32)

    def put_heads(ref, z, col):
        for j in range(IN_CHUNK // V_DIM):
            ref[col // V_DIM + j] = z[:, j * V_DIM:(j + 1) * V_DIM].astype(ref.dtype)

    for c in range(0, QK_WIDTH, IN_CHUNK):
        put_heads(q_ref, mm(c) * (SM_SCALE * LOG2E), c)
    for c in range(0, QK_WIDTH, IN_CHUNK):
        z = mm(_K_COL + c)
        k_ref[:, c:c + IN_CHUNK] = z
        put_heads(kb_ref, z, c)
    for c in range(0, D_MODEL, IN_CHUNK):
        z = mm(_V_COL + c)
        v_ref[:, c:c + IN_CHUNK] = z
        put_heads(vb_ref, z, c)
    for c in range(0, D_POOL, IN_CHUNK):
        u_ref[:, c:c + IN_CHUNK] = mm(_U_COL + c)
    for c in range(0, 2 * D_MODEL, IN_CHUNK):
        g_ref[:, c:c + IN_CHUNK] = mm(_G_COL + c).astype(BF16)


def _inproj_call(x, mod, w_in, layer, tm, tiles_per_row_group, head_dtype):
    m = x.shape[0]

    def row(i):
        return (i, 0)

    def heads(i):
        return (0, i, 0)

    head_spec = pl.BlockSpec((N_HEADS, tm, V_DIM), heads)
    head_shape = jax.ShapeDtypeStruct((N_HEADS, m, V_DIM), head_dtype)
    return pl.pallas_call(
        _inproj_kernel,
        grid=(m // tm,),
        in_specs=[pl.BlockSpec((tm, D_MODEL), row),
                  _mod_spec(mod, layer, tm, tiles_per_row_group, 1),
                  _mod_spec(mod, layer, tm, tiles_per_row_group, 0),
                  pl.BlockSpec((None, D_MODEL, D_IN), lambda i: (layer, 0, 0), pipeline_mode=pl.Buffered(1))],
        out_specs=[head_spec,
                   pl.BlockSpec((tm, QK_WIDTH), row),
                   pl.BlockSpec((tm, D_MODEL), row),
                   head_spec, head_spec,
                   pl.BlockSpec((tm, D_POOL), row),
                   pl.BlockSpec((tm, 2 * D_MODEL), row)],
        out_shape=[head_shape,
                   jax.ShapeDtypeStruct((m, QK_WIDTH), F32),
                   jax.ShapeDtypeStruct((m, D_MODEL), F32),
                   head_shape, head_shape,
                   jax.ShapeDtypeStruct((m, D_POOL), F32),
                   jax.ShapeDtypeStruct((m, 2 * D_MODEL), BF16)],
        compiler_params=_params(("arbitrary",)),
        name="inproj",
    )(x, mod, mod, w_in)


def _split_maps(q):
    lane = lax.broadcasted_iota(jnp.int32, q.shape, 1)
    qf = q.astype(F32)
    halves = [jnp.where(lane < HEAD_DIM, qf, 0.0), jnp.where(lane >= HEAD_DIM, qf, 0.0)]
    return jnp.concatenate(halves, axis=0).astype(BF16)


def _with_ones(v):
    return jnp.concatenate([v, jnp.ones_like(v)], axis=1)


def _softmax_update(blocks, m_scr, acc_scr):
    m_prev = m_scr[...]
    m_next = m_prev
    for s, off, _ in blocks:
        m_next = jnp.maximum(m_next, jnp.max(s, axis=1, keepdims=True) - off)
    pv = None
    for s, off, v in blocks:
        t = s.shape[1]
        shift = m_next + off
        shift = pltpu.repeat(shift, t // V_DIM, axis=1) if t >= V_DIM else shift[:, :t]
        p = jnp.exp2(s - shift).astype(BF16)
        d = jnp.dot(p, _with_ones(v), preferred_element_type=F32)
        pv = d if pv is None else pv + d
    alpha = jnp.exp2(m_prev - m_next)
    acc_scr[...] = pltpu.repeat(alpha, 2, axis=1) * acc_scr[...] + pv
    m_scr[...] = m_next


def _diff_finalize(lam_ref, g_ref, acc_scr, lam_init):
    n = acc_scr.shape[0] // 2
    acc = acc_scr[...]
    o = acc[:, :V_DIM] / acc[:, V_DIM:]
    lam = lam_ref[...]
    lam_val = (jnp.exp(jnp.sum(lam[0:1] * lam[1:2], axis=1, keepdims=True))
               - jnp.exp(jnp.sum(lam[2:3] * lam[3:4], axis=1, keepdims=True)) + lam_init)
    od = o[:n] - lam_val * o[n:]
    od = od * lax.rsqrt(jnp.mean(od * od, axis=1, keepdims=True) + EPS)
    return od * g_ref[...] * (1.0 - lam_init)


def _init_softmax_state(m_scr, acc_scr):
    m_scr[...] = jnp.full(m_scr.shape, -MASKED, F32)
    acc_scr[...] = jnp.zeros(acc_scr.shape, F32)


def _alibi_slopes_log2():
    return LOG2E * 2.0 ** (-8.0 * np.arange(1, N_HEADS + 1) / N_HEADS)


def _qk(q2, k):
    return lax.dot_general(q2, k, (((1,), (1,)), ((), ())), preferred_element_type=F32)


def _attn_prompt_kernel(qi_ref, ki_ref, slope_ref, q_ref, k_ref, v_ref, rel_ref, diag_ref, lam_ref, g_ref, o_ref,
                        q2_scr, m_scr, acc_scr, *, lam_init):
    h, p = pl.program_id(1), pl.program_id(2)
    qi, ki = qi_ref[p], ki_ref[p]
    t = q_ref.shape[0]

    @pl.when(ki == 0)
    def _():
        q2_scr[...] = _split_maps(q_ref[...])
        _init_softmax_state(m_scr, acc_scr)

    def step(bias_ref):
        s = (_qk(q2_scr[...], k_ref[...]).reshape(2, t, t) - bias_ref[...][None]).reshape(2 * t, t)
        off = slope_ref[h] * ((qi - ki) * t).astype(F32)
        _softmax_update([(s, off, v_ref[...])], m_scr, acc_scr)

    @pl.when(ki < qi)
    def _():
        step(rel_ref)

    @pl.when(ki == qi)
    def _():
        step(diag_ref)
        o_ref[...] = _diff_finalize(lam_ref, g_ref, acc_scr, lam_init)


def _prompt_bias_tables(t):
    r = lax.broadcasted_iota(jnp.int32, (t, t), 0)
    c = lax.broadcasted_iota(jnp.int32, (t, t), 1)
    rel = (r - c).astype(F32)[None] * jnp.asarray(_alibi_slopes_log2(), F32)[:, None, None]
    diag = jnp.where((r >= c)[None], rel, MASKED)
    return rel, diag


def _attn_prompt_call(q, kb, vb, rel, diag, slopes, lam, subln_g, layer, lam_init, batch, seq):
    t = min(ATTN_TILE, seq)
    nq = seq // t
    pairs = [(qi, ki) for qi in range(nq) for ki in range(qi + 1)]
    qi_tab = jnp.asarray([p[0] for p in pairs], jnp.int32)
    ki_tab = jnp.asarray([p[1] for p in pairs], jnp.int32)

    def q_map(b, h, p, qi_ref, ki_ref):
        return (h, b * nq + qi_ref[p], 0)

    def kv_map(b, h, p, qi_ref, ki_ref):
        return (h, b * nq + ki_ref[p], 0)

    def head(b, h, p, qi_ref, ki_ref):
        return (h, 0, 0)

    def lay(b, h, p, qi_ref, ki_ref):
        return (layer, 0, 0)

    grid_spec = pltpu.PrefetchScalarGridSpec(
        num_scalar_prefetch=2,
        grid=(batch, N_HEADS, len(pairs)),
        in_specs=[pl.BlockSpec(memory_space=pltpu.SMEM),
                  pl.BlockSpec((None, t, V_DIM), q_map),
                  pl.BlockSpec((None, t, V_DIM), kv_map),
                  pl.BlockSpec((None, t, V_DIM), kv_map),
                  pl.BlockSpec((None, t, t), head),
                  pl.BlockSpec((None, t, t), head),
                  pl.BlockSpec((None, 4, HEAD_DIM), lay),
                  pl.BlockSpec((None, 1, V_DIM), lay)],
        out_specs=pl.BlockSpec((None, t, V_DIM), q_map),
        scratch_shapes=[pltpu.VMEM((2 * t, V_DIM), BF16),
                        pltpu.VMEM((2 * t, V_DIM), F32),
                        pltpu.VMEM((2 * t, 2 * V_DIM), F32)])
    return pl.pallas_call(
        functools.partial(_attn_prompt_kernel, lam_init=lam_init),
        grid_spec=grid_spec,
        out_shape=jax.ShapeDtypeStruct((N_HEADS, batch * seq, V_DIM), F32),
        compiler_params=_params(("arbitrary",) * 3),
        name="attn_prompt",
    )(qi_tab, ki_tab, slopes, q, kb, vb, rel, diag, lam, subln_g.reshape(DEPTH, 1, V_DIM))


def _attn_sample_kernel(pt_ref, q_ref, kn_ref, vn_ref, *rest, lam_init, past_len, pages_per_step):
    del pt_ref
    pp = pages_per_step
    kp_refs, vp_refs = rest[:pp], rest[pp:2 * pp]
    bias_ref, biasn_ref, rs_ref, lam_ref, g_ref, o_ref, q2_scr, m_scr, acc_scr = rest[2 * pp:]
    j = pl.program_id(1)

    rows_new = q_ref.shape[0] * q_ref.shape[1]

    @pl.when(j == 0)
    def _():
        q2_scr[...] = _split_maps(q_ref[...].reshape(rows_new, V_DIM))
        _init_softmax_state(m_scr, acc_scr)

    q2 = q2_scr[...]
    blocks = []
    for i in range(pp):
        page_start = (j * pp + i) * PAGE_SIZE
        off = rs_ref[...] * (past_len - page_start).astype(F32)
        s = _qk(q2, kp_refs[i][...].astype(BF16)) - bias_ref[...]
        blocks.append((s, off, vp_refs[i][...].astype(BF16)))
    _softmax_update(blocks, m_scr, acc_scr)

    @pl.when(j == pl.num_programs(1) - 1)
    def _():
        kn = kn_ref[...].reshape(rows_new, V_DIM).astype(BF16)
        vn = vn_ref[...].reshape(rows_new, V_DIM).astype(BF16)
        s = _qk(q2, kn) - biasn_ref[...]
        _softmax_update([(s, 0.0, vn)], m_scr, acc_scr)
        o_ref[...] = _diff_finalize(lam_ref, g_ref, acc_scr, lam_init).reshape(o_ref.shape)


def _sample_bias_tables(dec_seq):
    rows = 2 * dec_seq * N_HEADS
    r = np.arange(rows)
    r_head, r_tok = (r // dec_seq) % N_HEADS, r % dec_seq
    slope = _alibi_slopes_log2()[r_head]
    c = np.arange(PAGE_SIZE * N_HEADS)
    c_head, c_pos = c % N_HEADS, c // N_HEADS
    same = r_head[:, None] == c_head[None, :]
    page = np.where(same, slope[:, None] * (r_tok[:, None] - c_pos[None, :]), MASKED)
    cn = np.arange(dec_seq * N_HEADS)
    cn_head, cn_tok = cn // dec_seq, cn % dec_seq
    ok = (r_head[:, None] == cn_head[None, :]) & (cn_tok[None, :] <= r_tok[:, None])
    new = np.where(ok, slope[:, None] * (r_tok[:, None] - cn_tok[None, :]), MASKED)
    row_slope = np.broadcast_to(slope[:, None], (rows, V_DIM))
    return (jnp.asarray(page, F32), jnp.asarray(new, F32), jnp.asarray(row_slope, F32))


def _attn_sample_call(page_table, q, kb, vb, cache_k4, cache_v4, tables, lam, subln_g, layer, lam_init, dec_seq):
    n_dec, n_pages = page_table.shape
    pp = min(PAGES_PER_STEP, n_pages)
    rows_new = dec_seq * N_HEADS
    rows_q = 2 * rows_new
    page_rows = PAGE_SIZE * N_HEADS
    bias_page, bias_new, row_slope = tables

    def seq_map(b, j, pt):
        return (0, b, 0)

    seq_spec = pl.BlockSpec((N_HEADS, dec_seq, V_DIM), seq_map)

    def page_map(i):
        return lambda b, j, pt: (layer, pt[b, j * pp + i], 0, 0)

    def const2(b, j, pt):
        return (0, 0)

    page_spec = [pl.BlockSpec((None, None, page_rows, V_DIM), page_map(i)) for i in range(pp)]
    grid_spec = pltpu.PrefetchScalarGridSpec(
        num_scalar_prefetch=1,
        grid=(n_dec, n_pages // pp),
        in_specs=[seq_spec, seq_spec, seq_spec,
                  *page_spec, *page_spec,
                  pl.BlockSpec((rows_q, page_rows), const2),
                  pl.BlockSpec((rows_q, rows_new), const2),
                  pl.BlockSpec((rows_q, V_DIM), const2),
                  pl.BlockSpec((None, 4, HEAD_DIM), lambda b, j, pt: (layer, 0, 0)),
                  pl.BlockSpec((None, 1, V_DIM), lambda b, j, pt: (layer, 0, 0))],
        out_specs=seq_spec,
        scratch_shapes=[pltpu.VMEM((rows_q, V_DIM), BF16),
                        pltpu.VMEM((rows_q, V_DIM), F32),
                        pltpu.VMEM((rows_q, 2 * V_DIM), F32)])
    return pl.pallas_call(
        functools.partial(_attn_sample_kernel, lam_init=lam_init, past_len=n_pages * PAGE_SIZE, pages_per_step=pp),
        grid_spec=grid_spec,
        out_shape=jax.ShapeDtypeStruct((N_HEADS, n_dec * dec_seq, V_DIM), F32),
        compiler_params=_params(("arbitrary", "arbitrary")),
        name="attn_sample",
    )(page_table, q, kb, vb, *([cache_k4] * pp), *([cache_v4] * pp),
      bias_page, bias_new, row_slope, lam, subln_g.reshape(DEPTH, 1, V_DIM))


def _merge_kernel(x_ref, o_ref, u_ref, halo_ref, g_ref, gm_ref, wp_ref, ps_ref, wo_ref, lg_ref, lb_ref,
                  y_ref, ext_scr, *, pos0, zero_first_halo):
    i = pl.program_id(0)
    a, r, _ = u_ref.shape
    tm = a * r
    hr = halo_ref.shape[1]

    halo = halo_ref[...]
    if zero_first_halo is not None:
        halo = jnp.where(i % zero_first_halo == 0, jnp.zeros_like(halo), halo)
    ext_scr[:, HALO - hr:HALO, :] = halo
    ext_scr[:, HALO:HALO + r, :] = u_ref[...]

    if zero_first_halo is not None:
        pos_base = pos0 + (i % zero_first_halo) * r
    else:
        pos_base = pos0
    pos = pos_base + lax.broadcasted_iota(jnp.int32, (a, r, POOL_GROUP_DIM), 1)
    pool_parts = []
    for g, w in enumerate(POOL_WINDOWS):
        lanes = slice(g * POOL_GROUP_DIM, (g + 1) * POOL_GROUP_DIM)
        tot = ext_scr[:, HALO:HALO + r, lanes]
        for k in range(1, w):
            tot = tot + ext_scr[:, HALO - k:HALO - k + r, lanes]
        cnt = jnp.minimum(pos + 1, w).astype(F32)
        pooled = tot / cnt - ext_scr[:, HALO:HALO + r, lanes]
        pool_parts.append(jnp.dot(pooled.reshape(tm, POOL_GROUP_DIM).astype(BF16), wp_ref[g],
                                  preferred_element_type=F32))
    pool_o = jnp.concatenate(pool_parts, axis=1) * ps_ref[...]

    gates = g_ref[...].astype(F32)
    attn_o = jnp.concatenate([o_ref[j] for j in range(N_HEADS)], axis=1)
    mixed = (jax.nn.sigmoid(gates[:, :D_MODEL]) * attn_o
             + jax.nn.sigmoid(gates[:, D_MODEL:]) * pool_o)
    out = jnp.dot(mixed.astype(BF16), wo_ref[...], preferred_element_type=F32)
    y = ALPHA * x_ref[...] + gm_ref[...] * out
    y_ref[...] = _layer_norm(y, lg_ref[...], lb_ref[...])


def _merge_call(x, o, u3, halo_src, halo_spec, gates, mod, w_pool, pool_scale, w_out, ln_g, ln_b,
                layer, a_blk, tiles_per_row_group, pos0, zero_first_halo):
    m = x.shape[0]
    n_outer, rows, _ = u3.shape
    r_blk = rows // tiles_per_row_group
    tm = a_blk * r_blk

    def row(i):
        return (i, 0)

    def vec(i):
        return (layer, 0, 0)

    if tiles_per_row_group > 1:
        u_spec = pl.BlockSpec((a_blk, r_blk, D_POOL), lambda i: (i // tiles_per_row_group, i % tiles_per_row_group, 0))
    else:
        u_spec = pl.BlockSpec((a_blk, r_blk, D_POOL), lambda i: (i, 0, 0))

    return pl.pallas_call(
        functools.partial(_merge_kernel, pos0=pos0, zero_first_halo=zero_first_halo),
        grid=(m // tm,),
        in_specs=[pl.BlockSpec((tm, D_MODEL), row),
                  pl.BlockSpec((N_HEADS, tm, V_DIM), lambda i: (0, i, 0)),
                  u_spec,
                  halo_spec,
                  pl.BlockSpec((tm, 2 * D_MODEL), row),
                  _mod_spec(mod, layer, tm, tiles_per_row_group, 2),
                  pl.BlockSpec((None, N_POOL_GROUPS, POOL_GROUP_DIM, POOL_OUT_DIM), lambda i: (layer, 0, 0, 0)),
                  pl.BlockSpec((None, 1, D_MODEL), vec),
                  pl.BlockSpec((None, D_MODEL, D_MODEL), vec),
                  pl.BlockSpec((None, 1, D_MODEL), vec),
                  pl.BlockSpec((None, 1, D_MODEL), vec)],
        out_specs=pl.BlockSpec((tm, D_MODEL), row),
        out_shape=jax.ShapeDtypeStruct(x.shape, F32),
        scratch_shapes=[pltpu.VMEM((a_blk, HALO + r_blk, D_POOL), F32)],
        compiler_params=_params(("arbitrary",)),
        name="merge",
    )(x, o, u3, halo_src, gates, mod, w_pool, pool_scale.reshape(DEPTH, 1, D_MODEL), w_out,
      ln_g.reshape(DEPTH, 1, D_MODEL), ln_b.reshape(DEPTH, 1, D_MODEL))


def _ffn_kernel(x_ref, sc_ref, sh_ref, gf_ref, wgu_ref, wd_ref, lg_ref, lb_ref, y_ref):
    x = x_ref[...]
    h = (x * (1.0 + sc_ref[...]) + sh_ref[...]).astype(BF16)
    acc = None
    for f0, f1 in FF_CHUNKS:
        a = jnp.dot(h, wgu_ref[:, f0:f1], preferred_element_type=F32)
        b = jnp.dot(h, wgu_ref[:, D_FF + f0:D_FF + f1], preferred_element_type=F32)
        act = (a * jax.nn.sigmoid(a) * b).astype(BF16)
        d = jnp.dot(act, wd_ref[f0:f1, :], preferred_element_type=F32)
        acc = d if acc is None else acc + d
    y = ALPHA * x + gf_ref[...] * acc
    y_ref[...] = _layer_norm(y, lg_ref[...], lb_ref[...])


def _ffn_call(x, mod, w_gu, w_down, ln_g, ln_b, layer, tm, tiles_per_row_group):
    m = x.shape[0]

    def row(i):
        return (i, 0)

    def vec(i):
        return (layer, 0, 0)

    return pl.pallas_call(
        _ffn_kernel,
        grid=(m // tm,),
        in_specs=[pl.BlockSpec((tm, D_MODEL), row),
                  _mod_spec(mod, layer, tm, tiles_per_row_group, 4),
                  _mod_spec(mod, layer, tm, tiles_per_row_group, 3),
                  _mod_spec(mod, layer, tm, tiles_per_row_group, 5),
                  pl.BlockSpec((None, D_MODEL, 2 * D_FF), vec, pipeline_mode=pl.Buffered(1)),
                  pl.BlockSpec((None, D_FF, D_MODEL), vec, pipeline_mode=pl.Buffered(1)),
                  pl.BlockSpec((None, 1, D_MODEL), vec),
                  pl.BlockSpec((None, 1, D_MODEL), vec)],
        out_specs=pl.BlockSpec((tm, D_MODEL), row),
        out_shape=jax.ShapeDtypeStruct(x.shape, F32),
        compiler_params=_params(("arbitrary",)),
        name="ffn",
    )(x, mod, mod, mod, w_gu, w_down, ln_g.reshape(DEPTH, 1, D_MODEL), ln_b.reshape(DEPTH, 1, D_MODEL))


def kernel(x_prompt, x_sample, cache_k, cache_v, state_pool, page_table, c_prompt, c_sample, w_in, lam, subln_g,
           w_pool, pool_scale, w_out, w_ada, b_ada, ln1_g, ln1_b, ln2_g, ln2_b, w_gu, w_down):
    batch, seq, _ = x_prompt.shape
    n_dec, dec_seq, _ = x_sample.shape
    n_pages = page_table.shape[1]
    n_pool_pages = cache_k.shape[1]
    past_len = n_pages * PAGE_SIZE

    w_in_b, w_out_b, w_ada_b, w_pool_b = (w.astype(BF16) for w in (w_in, w_out, w_ada, w_pool))
    w_gu_b, w_down_b = w_gu.astype(BF16), w_down.astype(BF16)

    ada_p, ada_s = _ada_call(c_prompt, jnp.repeat(c_sample, dec_seq, axis=0), w_ada_b, b_ada)
    mod_p = ada_p.reshape(DEPTH * batch, 1, 6 * D_MODEL)
    mod_s = ada_s.reshape(DEPTH * n_dec * dec_seq, 6 * D_MODEL)

    cache_k4 = cache_k.reshape(DEPTH, n_pool_pages, PAGE_SIZE * N_HEADS, V_DIM)
    cache_v4 = cache_v.reshape(DEPTH, n_pool_pages, PAGE_SIZE * N_HEADS, V_DIM)

    rel, diag = _prompt_bias_tables(min(ATTN_TILE, seq))
    slopes = jnp.asarray(_alibi_slopes_log2(), F32)
    sample_tables = _sample_bias_tables(dec_seq)

    p_in, p_mg, p_ff = min(512, seq), min(512, seq), min(512, seq)
    m_s = n_dec * dec_seq
    s_in, s_ff = min(256, m_s), min(256, m_s)
    s_mg_seqs = min(32, n_dec)
    halo_blocks = p_mg // HALO
    mg_tiles = seq // p_mg

    xp = x_prompt.reshape(batch * seq, D_MODEL)
    xs = x_sample.reshape(m_s, D_MODEL)
    kp, vp, bp, ks, vs, bs = [], [], [], [], [], []
    for layer in range(DEPTH):
        lam_init = 0.8 - 0.6 * math.exp(-0.3 * layer)

        q, k, v, kb, vb, u, gates = _inproj_call(xp, mod_p, w_in_b, layer, p_in, seq // p_in, BF16)
        o = _attn_prompt_call(q, kb, vb, rel, diag, slopes, lam, subln_g, layer, lam_init, batch, seq)
        u3 = u.reshape(batch, seq, D_POOL)
        halo_spec = pl.BlockSpec(
            (1, HALO, D_POOL),
            lambda i: (i // mg_tiles, jnp.maximum((i % mg_tiles) * halo_blocks - 1, 0), 0))
        xp = _merge_call(xp, o, u3, u3, halo_spec, gates, mod_p, w_pool_b, pool_scale, w_out_b, ln1_g, ln1_b,
                         layer, 1, mg_tiles, 0, mg_tiles)
        xp = _ffn_call(xp, mod_p, w_gu_b, w_down_b, ln2_g, ln2_b, layer, p_ff, seq // p_ff)
        kp.append(k.reshape(batch, seq, N_HEADS, V_DIM))
        vp.append(v.reshape(batch, seq, N_HEADS, V_DIM))
        bp.append(u3[:, seq - POOL_BUF:])

        q, k, v, kb, vb, u, gates = _inproj_call(xs, mod_s, w_in_b, layer, s_in, 1, F32)
        o = _attn_sample_call(page_table, q, kb, vb, cache_k4, cache_v4, sample_tables, lam, subln_g,
                              layer, lam_init, dec_seq)
        us = u.reshape(n_dec, dec_seq, D_POOL)
        halo_spec = pl.BlockSpec((None, s_mg_seqs, POOL_BUF, D_POOL), lambda i: (layer, i, 0, 0))
        xs = _merge_call(xs, o, us, state_pool, halo_spec, gates, mod_s, w_pool_b, pool_scale, w_out_b,
                         ln1_g, ln1_b, layer, s_mg_seqs, 1, past_len, None)
        xs = _ffn_call(xs, mod_s, w_gu_b, w_down_b, ln2_g, ln2_b, layer, s_ff, 1)
        ks.append(k.reshape(n_dec, dec_seq, N_HEADS, V_DIM))
        vs.append(v.reshape(n_dec, dec_seq, N_HEADS, V_DIM))
        bs.append(jnp.concatenate([state_pool[layer], us], axis=1)[:, -POOL_BUF:])

    return (xp.reshape(batch, seq, D_MODEL), xs.reshape(n_dec, dec_seq, D_MODEL),
            jnp.stack(kp), jnp.stack(vp), jnp.stack(bp), jnp.stack(ks), jnp.stack(vs), jnp.stack(bs))
```

```python
import functools
import math

import numpy as np
import jax
import jax.numpy as jnp
from jax import lax
from jax.experimental import pallas as pl
from jax.experimental.pallas import tpu as pltpu

D_MODEL = 1024
DEPTH = 4
PAGE_SIZE = 128
HEAD_DIM = 64
V_DIM = 2 * HEAD_DIM
N_HEADS = D_MODEL // V_DIM
QK_WIDTH = N_HEADS * 2 * HEAD_DIM
SM_SCALE = HEAD_DIM ** -0.5
LOG2E = math.log2(math.e)
POOL_WINDOWS = (2, 4, 8, 16)
N_POOL_GROUPS = len(POOL_WINDOWS)
D_POOL = D_MODEL // 2
POOL_GROUP_DIM = D_POOL // N_POOL_GROUPS
POOL_OUT_DIM = D_MODEL // N_POOL_GROUPS
POOL_BUF = max(POOL_WINDOWS) - 1
HALO = POOL_BUF + 1
D_FF = -((-8 * D_MODEL) // (3 * 256)) * 256
D_IN = 2 * QK_WIDTH + D_MODEL + D_POOL + 2 * D_MODEL
ALPHA = (2.0 * DEPTH) ** 0.25
EPS = 1e-5
MASKED = 1e30

V7X_VMEM_LIMIT_BYTES = 56 * 1024 * 1024
IN_CHUNK = 512
FF_CHUNKS = ((0, 1280), (1280, D_FF))
ATTN_TILE = 512
ATTN_ROW_CHUNKS = 8
PAGES_PER_STEP = 8

F32 = jnp.float32
BF16 = jnp.bfloat16


def _params(semantics):
    return pltpu.CompilerParams(dimension_semantics=semantics, vmem_limit_bytes=V7X_VMEM_LIMIT_BYTES)


def _layer_norm(y, g, b):
    mu = jnp.mean(y, axis=-1, keepdims=True)
    yc = y - mu
    var = jnp.mean(yc * yc, axis=-1, keepdims=True)
    return yc * lax.rsqrt(var + EPS) * g + b


def _mod_spec(mod, layer, tm, tiles_per_row_group, which):
    if mod.ndim == 3:
        base = layer * (mod.shape[0] // DEPTH)
        return pl.BlockSpec((None, 1, D_MODEL), lambda i, *_: (base + i // tiles_per_row_group, 0, which))
    base = layer * (mod.shape[0] // DEPTH // tm)
    return pl.BlockSpec((tm, D_MODEL), lambda i, *_: (base + i, which))


def _ada_kernel(cp_ref, cs_ref, w_ref, b_ref, op_ref, os_ref):
    w = w_ref[...]
    for c_ref, o_ref in ((cp_ref, op_ref), (cs_ref, os_ref)):
        c = c_ref[...]
        s = (c * jax.nn.sigmoid(c)).astype(BF16)
        o_ref[...] = jnp.dot(s, w, preferred_element_type=F32) + b_ref[...]


def _ada_call(c_p, c_s, w_ada, b_ada):
    n_p, n_s = c_p.shape[0], c_s.shape[0]
    tn = 1024
    return pl.pallas_call(
        _ada_kernel,
        grid=(DEPTH, 6 * D_MODEL // tn),
        in_specs=[pl.BlockSpec((n_p, D_MODEL), lambda l, j: (0, 0)),
                  pl.BlockSpec((n_s, D_MODEL), lambda l, j: (0, 0)),
                  pl.BlockSpec((None, D_MODEL, tn), lambda l, j: (l, 0, j)),
                  pl.BlockSpec((None, 1, tn), lambda l, j: (l, 0, j))],
        out_specs=[pl.BlockSpec((None, n_p, tn), lambda l, j: (l, 0, j)),
                   pl.BlockSpec((None, n_s, tn), lambda l, j: (l, 0, j))],
        out_shape=(jax.ShapeDtypeStruct((DEPTH, n_p, 6 * D_MODEL), F32),
                   jax.ShapeDtypeStruct((DEPTH, n_s, 6 * D_MODEL), F32)),
        compiler_params=_params(("arbitrary", "arbitrary")),
        name="ada",
    )(c_p, c_s, w_ada, b_ada.reshape(DEPTH, 1, 6 * D_MODEL))


_K_COL, _V_COL, _U_COL, _G_COL = QK_WIDTH, 2 * QK_WIDTH, 2 * QK_WIDTH + D_MODEL, 2 * QK_WIDTH + D_MODEL + D_POOL


def _inproj_kernel(x_ref, sc_ref, sh_ref, w_ref, *rest):
    q_ref, k_ref, v_ref, kb_ref, vb_ref, u_ref, g_ref = rest[-7:]
    tm = x_ref.shape[0]
    h = (x_ref[...] * (1.0 + sc_ref[...]) + sh_ref[...]).astype(BF16)

    def mm(col):
        return jnp.dot(h, w_ref[:, col:col + IN_CHUNK], preferred_element_type=F32)

    def put_heads(ref, z, col):
        for j in range(IN_CHUNK // V_DIM):
            ref[col // V_DIM + j] = z[:, j * V_DIM:(j + 1) * V_DIM].astype(ref.dtype)

    def put_rows(ref, z, col):
        for j in range(IN_CHUNK // V_DIM):
            ref[pl.ds(col // V_DIM + j, tm, stride=N_HEADS), :] = z[:, j * V_DIM:(j + 1) * V_DIM]

    for c in range(0, QK_WIDTH, IN_CHUNK):
        put_heads(q_ref, mm(c) * (SM_SCALE * LOG2E), c)
    for c in range(0, QK_WIDTH, IN_CHUNK):
        z = mm(_K_COL + c)
        put_rows(k_ref, z, c)
        put_heads(kb_ref, z, c)
    for c in range(0, D_MODEL, IN_CHUNK):
        z = mm(_V_COL + c)
        put_rows(v_ref, z, c)
        put_heads(vb_ref, z, c)
    for c in range(0, D_POOL, IN_CHUNK):
        u_ref[:, c:c + IN_CHUNK] = mm(_U_COL + c)
    for c in range(0, 2 * D_MODEL, IN_CHUNK):
        g_ref[:, c:c + IN_CHUNK] = mm(_G_COL + c).astype(BF16)


def _inproj_call(x, mod, w_in, layer, tm, tiles_per_row_group, head_dtype, kv_stacks):
    m = x.shape[0]

    def row(i):
        return (i, 0)

    def heads(i):
        return (0, i, 0)

    head_spec = pl.BlockSpec((N_HEADS, tm, V_DIM), heads)
    head_shape = jax.ShapeDtypeStruct((N_HEADS, m, V_DIM), head_dtype)
    stack_spec = pl.BlockSpec((None, tm * N_HEADS, V_DIM), lambda i: (layer, i, 0))
    stack_shape = jax.ShapeDtypeStruct((DEPTH, m * N_HEADS, V_DIM), F32)
    in_specs = [pl.BlockSpec((tm, D_MODEL), row),
                _mod_spec(mod, layer, tm, tiles_per_row_group, 1),
                _mod_spec(mod, layer, tm, tiles_per_row_group, 0),
                pl.BlockSpec((None, D_MODEL, D_IN), lambda i: (layer, 0, 0), pipeline_mode=pl.Buffered(1))]
    args = [x, mod, mod, w_in]
    aliases = {}
    if kv_stacks is not None:
        in_specs += [pl.BlockSpec(memory_space=pl.ANY)] * 2
        args += list(kv_stacks)
        aliases = {4: 1, 5: 2}
    q, k_stack, v_stack, kh, vh, u, gates = pl.pallas_call(
        _inproj_kernel,
        grid=(m // tm,),
        in_specs=in_specs,
        out_specs=[head_spec, stack_spec, stack_spec, head_spec, head_spec,
                   pl.BlockSpec((tm, D_POOL), row),
                   pl.BlockSpec((tm, 2 * D_MODEL), row)],
        out_shape=[head_shape, stack_shape, stack_shape, head_shape, head_shape,
                   jax.ShapeDtypeStruct((m, D_POOL), F32),
                   jax.ShapeDtypeStruct((m, 2 * D_MODEL), BF16)],
        input_output_aliases=aliases,
        compiler_params=_params(("arbitrary",)),
        name="inproj",
    )(*args)
    return q, kh, vh, u, gates, (k_stack, v_stack)


def _split_maps(q):
    lane = lax.broadcasted_iota(jnp.int32, q.shape, 1)
    qf = q.astype(F32)
    halves = [jnp.where(lane < HEAD_DIM, qf, 0.0), jnp.where(lane >= HEAD_DIM, qf, 0.0)]
    return jnp.concatenate(halves, axis=0).astype(BF16)


def _with_ones(v):
    return jnp.concatenate([v, jnp.ones_like(v)], axis=1)


def _lane_repeat(x, n):
    return x if n == 1 else jnp.concatenate([x] * n, axis=1)


def _softmax_update(blocks, m_scr, acc_scr, rows=None):
    rs = slice(None) if rows is None else slice(rows[0], rows[1])
    m_prev = m_scr[rs, :]
    m_next = m_prev
    for s, off, _ in blocks:
        m_next = jnp.maximum(m_next, jnp.max(s, axis=1, keepdims=True) - off)
    pv = None
    for s, off, v in blocks:
        t = s.shape[1]
        shift = m_next + off
        shift = _lane_repeat(shift, t // V_DIM) if t >= V_DIM else shift[:, :t]
        p = jnp.exp2(s - shift).astype(BF16)
        d = jnp.dot(p, _with_ones(v), preferred_element_type=F32)
        pv = d if pv is None else pv + d
    alpha = jnp.exp2(m_prev - m_next)
    acc_scr[rs, :] = _lane_repeat(alpha, 2) * acc_scr[rs, :] + pv
    m_scr[rs, :] = m_next


def _diff_finalize(lam_ref, g_ref, acc_scr, lam_init):
    n = acc_scr.shape[0] // 2
    acc = acc_scr[...]
    o = acc[:, :V_DIM] / acc[:, V_DIM:]
    lam = lam_ref[...]
    lam_val = (jnp.exp(jnp.sum(lam[0:1] * lam[1:2], axis=1, keepdims=True))
               - jnp.exp(jnp.sum(lam[2:3] * lam[3:4], axis=1, keepdims=True)) + lam_init)
    od = o[:n] - lam_val * o[n:]
    od = od * lax.rsqrt(jnp.mean(od * od, axis=1, keepdims=True) + EPS)
    return od * g_ref[...] * (1.0 - lam_init)


def _init_softmax_state(m_scr, acc_scr):
    m_scr[...] = jnp.full(m_scr.shape, -MASKED, F32)
    acc_scr[...] = jnp.zeros(acc_scr.shape, F32)


def _alibi_slopes_log2():
    return LOG2E * 2.0 ** (-8.0 * np.arange(1, N_HEADS + 1) / N_HEADS)


def _qk(q2, k):
    return lax.dot_general(q2, k, (((1,), (1,)), ((), ())), preferred_element_type=F32)


def _attn_prompt_kernel(slope_ref, q_ref, k_ref, v_ref, rel_ref, diag_ref, lam_ref, g_ref, o_ref,
                        q2_scr, m_scr, acc_scr, *, lam_init):
    h, qi = pl.program_id(1), pl.program_id(2)
    t = q_ref.shape[0]
    q2_scr[...] = _split_maps(q_ref[...])
    _init_softmax_state(m_scr, acc_scr)

    def step(ki, bias_ref):
        off = slope_ref[h] * ((qi - ki) * t).astype(F32)
        rows = pl.ds(pl.multiple_of(ki * t, t), t)
        k, v = k_ref[rows, :], v_ref[rows, :]
        rc = (2 * t) // ATTN_ROW_CHUNKS
        for c in range(ATTN_ROW_CHUNKS):
            r0 = c * rc
            b0 = r0 % t
            s = _qk(q2_scr[r0:r0 + rc, :], k) - bias_ref[b0:b0 + rc, :]
            _softmax_update([(s, off, v)], m_scr, acc_scr, (r0, r0 + rc))

    def below_diagonal(ki, carry):
        step(ki, rel_ref)
        return carry

    lax.fori_loop(0, qi, below_diagonal, 0)
    step(qi, diag_ref)
    o_ref[...] = _diff_finalize(lam_ref, g_ref, acc_scr, lam_init)


def _prompt_bias_tables(t):
    r = lax.broadcasted_iota(jnp.int32, (t, t), 0)
    c = lax.broadcasted_iota(jnp.int32, (t, t), 1)
    rel = (r - c).astype(F32)[None] * jnp.asarray(_alibi_slopes_log2(), F32)[:, None, None]
    diag = jnp.where((r >= c)[None], rel, MASKED)
    return rel, diag


def _attn_prompt_call(q, kb, vb, rel, diag, slopes, lam, subln_g, layer, lam_init, batch, seq):
    t = min(ATTN_TILE, seq)
    nq = seq // t

    def q_map(b, h, qi):
        return (h, b * nq + qi, 0)

    def kv_map(b, h, qi):
        return (h, b, 0)

    def head(b, h, qi):
        return (h, 0, 0)

    def lay(b, h, qi):
        return (layer, 0, 0)

    return pl.pallas_call(
        functools.partial(_attn_prompt_kernel, lam_init=lam_init),
        grid=(batch, N_HEADS, nq),
        in_specs=[pl.BlockSpec(memory_space=pltpu.SMEM),
                  pl.BlockSpec((None, t, V_DIM), q_map),
                  pl.BlockSpec((None, seq, V_DIM), kv_map),
                  pl.BlockSpec((None, seq, V_DIM), kv_map),
                  pl.BlockSpec((None, t, t), head),
                  pl.BlockSpec((None, t, t), head),
                  pl.BlockSpec((None, 4, HEAD_DIM), lay),
                  pl.BlockSpec((None, 1, V_DIM), lay)],
        out_specs=pl.BlockSpec((None, t, V_DIM), q_map),
        out_shape=jax.ShapeDtypeStruct((N_HEADS, batch * seq, V_DIM), F32),
        scratch_shapes=[pltpu.VMEM((2 * t, V_DIM), BF16),
                        pltpu.VMEM((2 * t, V_DIM), F32),
                        pltpu.VMEM((2 * t, 2 * V_DIM), F32)],
        compiler_params=_params(("arbitrary",) * 3),
        name="attn_prompt",
    )(slopes, q, kb, vb, rel, diag, lam, subln_g.reshape(DEPTH, 1, V_DIM))


def _attn_sample_kernel(pt_ref, q_ref, kn_ref, vn_ref, *rest, lam_init, past_len, pages_per_step):
    del pt_ref
    pp = pages_per_step
    kp_refs, vp_refs = rest[:pp], rest[pp:2 * pp]
    bias_ref, biasn_ref, rs_ref, lam_ref, g_ref, o_ref, q2_scr, m_scr, acc_scr = rest[2 * pp:]
    j = pl.program_id(1)

    rows_new = q_ref.shape[0] * q_ref.shape[1]

    @pl.when(j == 0)
    def _():
        q2_scr[...] = _split_maps(q_ref[...].reshape(rows_new, V_DIM))
        _init_softmax_state(m_scr, acc_scr)

    q2 = q2_scr[...]
    blocks = []
    for i in range(pp):
        page_start = (j * pp + i) * PAGE_SIZE
        off = rs_ref[...] * (past_len - page_start).astype(F32)
        s = _qk(q2, kp_refs[i][...].astype(BF16)) - bias_ref[...]
        blocks.append((s, off, vp_refs[i][...].astype(BF16)))
    _softmax_update(blocks, m_scr, acc_scr)

    @pl.when(j == pl.num_programs(1) - 1)
    def _():
        kn = kn_ref[...].reshape(rows_new, V_DIM).astype(BF16)
        vn = vn_ref[...].reshape(rows_new, V_DIM).astype(BF16)
        s = _qk(q2, kn) - biasn_ref[...]
        _softmax_update([(s, 0.0, vn)], m_scr, acc_scr)
        o_ref[...] = _diff_finalize(lam_ref, g_ref, acc_scr, lam_init).reshape(o_ref.shape)


def _sample_bias_tables(dec_seq):
    rows = 2 * dec_seq * N_HEADS
    r = np.arange(rows)
    r_head, r_tok = (r // dec_seq) % N_HEADS, r % dec_seq
    slope = _alibi_slopes_log2()[r_head]
    c = np.arange(PAGE_SIZE * N_HEADS)
    c_head, c_pos = c % N_HEADS, c // N_HEADS
    same = r_head[:, None] == c_head[None, :]
    page = np.where(same, slope[:, None] * (r_tok[:, None] - c_pos[None, :]), MASKED)
    cn = np.arange(dec_seq * N_HEADS)
    cn_head, cn_tok = cn // dec_seq, cn % dec_seq
    ok = (r_head[:, None] == cn_head[None, :]) & (cn_tok[None, :] <= r_tok[:, None])
    new = np.where(ok, slope[:, None] * (r_tok[:, None] - cn_tok[None, :]), MASKED)
    row_slope = np.broadcast_to(slope[:, None], (rows, V_DIM))
    return (jnp.asarray(page, F32), jnp.asarray(new, F32), jnp.asarray(row_slope, F32))


def _attn_sample_call(page_table, q, kb, vb, cache_k4, cache_v4, tables, lam, subln_g, layer, lam_init, dec_seq):
    n_dec, n_pages = page_table.shape
    pp = min(PAGES_PER_STEP, n_pages)
    rows_new = dec_seq * N_HEADS
    rows_q = 2 * rows_new
    page_rows = PAGE_SIZE * N_HEADS
    bias_page, bias_new, row_slope = tables

    def seq_map(b, j, pt):
        return (0, b, 0)

    seq_spec = pl.BlockSpec((N_HEADS, dec_seq, V_DIM), seq_map)

    def page_map(i):
        return lambda b, j, pt: (layer, pt[b, j * pp + i], 0, 0)

    def const2(b, j, pt):
        return (0, 0)

    page_spec = [pl.BlockSpec((None, None, page_rows, V_DIM), page_map(i)) for i in range(pp)]
    grid_spec = pltpu.PrefetchScalarGridSpec(
        num_scalar_prefetch=1,
        grid=(n_dec, n_pages // pp),
        in_specs=[seq_spec, seq_spec, seq_spec,
                  *page_spec, *page_spec,
                  pl.BlockSpec((rows_q, page_rows), const2),
                  pl.BlockSpec((rows_q, rows_new), const2),
                  pl.BlockSpec((rows_q, V_DIM), const2),
                  pl.BlockSpec((None, 4, HEAD_DIM), lambda b, j, pt: (layer, 0, 0)),
                  pl.BlockSpec((None, 1, V_DIM), lambda b, j, pt: (layer, 0, 0))],
        out_specs=seq_spec,
        scratch_shapes=[pltpu.VMEM((rows_q, V_DIM), BF16),
                        pltpu.VMEM((rows_q, V_DIM), F32),
                        pltpu.VMEM((rows_q, 2 * V_DIM), F32)])
    return pl.pallas_call(
        functools.partial(_attn_sample_kernel, lam_init=lam_init, past_len=n_pages * PAGE_SIZE, pages_per_step=pp),
        grid_spec=grid_spec,
        out_shape=jax.ShapeDtypeStruct((N_HEADS, n_dec * dec_seq, V_DIM), F32),
        compiler_params=_params(("arbitrary", "arbitrary")),
        name="attn_sample",
    )(page_table, q, kb, vb, *([cache_k4] * pp), *([cache_v4] * pp),
      bias_page, bias_new, row_slope, lam, subln_g.reshape(DEPTH, 1, V_DIM))


def _merge_kernel(x_ref, o_ref, u_ref, halo_ref, g_ref, gm_ref, wp_ref, ps_ref, wo_ref, lg_ref, lb_ref,
                  y_ref, ext_scr, *, pos0, zero_first_halo):
    i = pl.program_id(0)
    a, r, _ = u_ref.shape
    tm = a * r
    hr = halo_ref.shape[1]

    halo = halo_ref[...]
    if zero_first_halo is not None:
        halo = jnp.where(i % zero_first_halo == 0, jnp.zeros_like(halo), halo)
    ext_scr[:, HALO - hr:HALO, :] = halo
    ext_scr[:, HALO:HALO + r, :] = u_ref[...]

    if zero_first_halo is not None:
        pos_base = pos0 + (i % zero_first_halo) * r
    else:
        pos_base = pos0
    pos = pos_base + lax.broadcasted_iota(jnp.int32, (a, r, POOL_GROUP_DIM), 1)
    pool_parts = []
    for g, w in enumerate(POOL_WINDOWS):
        lanes = slice(g * POOL_GROUP_DIM, (g + 1) * POOL_GROUP_DIM)
        tot = ext_scr[:, HALO:HALO + r, lanes]
        for k in range(1, w):
            tot = tot + ext_scr[:, HALO - k:HALO - k + r, lanes]
        cnt = jnp.minimum(pos + 1, w).astype(F32)
        pooled = tot / cnt - ext_scr[:, HALO:HALO + r, lanes]
        pool_parts.append(jnp.dot(pooled.reshape(tm, POOL_GROUP_DIM).astype(BF16), wp_ref[g],
                                  preferred_element_type=F32))
    pool_o = jnp.concatenate(pool_parts, axis=1) * ps_ref[...]

    gates = g_ref[...].astype(F32)
    attn_o = jnp.concatenate([o_ref[j] for j in range(N_HEADS)], axis=1)
    mixed = (jax.nn.sigmoid(gates[:, :D_MODEL]) * attn_o
             + jax.nn.sigmoid(gates[:, D_MODEL:]) * pool_o)
    out = jnp.dot(mixed.astype(BF16), wo_ref[...], preferred_element_type=F32)
    y = ALPHA * x_ref[...] + gm_ref[...] * out
    y_ref[...] = _layer_norm(y, lg_ref[...], lb_ref[...])


def _merge_call(x, o, u3, halo_src, halo_spec, gates, mod, w_pool, pool_scale, w_out, ln_g, ln_b,
                layer, a_blk, tiles_per_row_group, pos0, zero_first_halo):
    m = x.shape[0]
    n_outer, rows, _ = u3.shape
    r_blk = rows // tiles_per_row_group
    tm = a_blk * r_blk

    def row(i):
        return (i, 0)

    def vec(i):
        return (layer, 0, 0)

    if tiles_per_row_group > 1:
        u_spec = pl.BlockSpec((a_blk, r_blk, D_POOL), lambda i: (i // tiles_per_row_group, i % tiles_per_row_group, 0))
    else:
        u_spec = pl.BlockSpec((a_blk, r_blk, D_POOL), lambda i: (i, 0, 0))

    return pl.pallas_call(
        functools.partial(_merge_kernel, pos0=pos0, zero_first_halo=zero_first_halo),
        grid=(m // tm,),
        in_specs=[pl.BlockSpec((tm, D_MODEL), row),
                  pl.BlockSpec((N_HEADS, tm, V_DIM), lambda i: (0, i, 0)),
                  u_spec,
                  halo_spec,
                  pl.BlockSpec((tm, 2 * D_MODEL), row),
                  _mod_spec(mod, layer, tm, tiles_per_row_group, 2),
                  pl.BlockSpec((None, N_POOL_GROUPS, POOL_GROUP_DIM, POOL_OUT_DIM), lambda i: (layer, 0, 0, 0)),
                  pl.BlockSpec((None, 1, D_MODEL), vec),
                  pl.BlockSpec((None, D_MODEL, D_MODEL), vec),
                  pl.BlockSpec((None, 1, D_MODEL), vec),
                  pl.BlockSpec((None, 1, D_MODEL), vec)],
        out_specs=pl.BlockSpec((tm, D_MODEL), row),
        out_shape=jax.ShapeDtypeStruct(x.shape, F32),
        scratch_shapes=[pltpu.VMEM((a_blk, HALO + r_blk, D_POOL), F32)],
        compiler_params=_params(("arbitrary",)),
        name="merge",
    )(x, o, u3, halo_src, gates, mod, w_pool, pool_scale.reshape(DEPTH, 1, D_MODEL), w_out,
      ln_g.reshape(DEPTH, 1, D_MODEL), ln_b.reshape(DEPTH, 1, D_MODEL))


def _ffn_kernel(x_ref, sc_ref, sh_ref, gf_ref, wgu_ref, wd_ref, lg_ref, lb_ref, y_ref):
    x = x_ref[...]
    h = (x * (1.0 + sc_ref[...]) + sh_ref[...]).astype(BF16)
    acc = None
    for f0, f1 in FF_CHUNKS:
        a = jnp.dot(h, wgu_ref[:, f0:f1], preferred_element_type=F32)
        b = jnp.dot(h, wgu_ref[:, D_FF + f0:D_FF + f1], preferred_element_type=F32)
        act = (a * jax.nn.sigmoid(a) * b).astype(BF16)
        d = jnp.dot(act, wd_ref[f0:f1, :], preferred_element_type=F32)
        acc = d if acc is None else acc + d
    y = ALPHA * x + gf_ref[...] * acc
    y_ref[...] = _layer_norm(y, lg_ref[...], lb_ref[...])


def _ffn_call(x, mod, w_gu, w_down, ln_g, ln_b, layer, tm, tiles_per_row_group):
    m = x.shape[0]

    def row(i):
        return (i, 0)

    def vec(i):
        return (layer, 0, 0)

    return pl.pallas_call(
        _ffn_kernel,
        grid=(m // tm,),
        in_specs=[pl.BlockSpec((tm, D_MODEL), row),
                  _mod_spec(mod, layer, tm, tiles_per_row_group, 4),
                  _mod_spec(mod, layer, tm, tiles_per_row_group, 3),
                  _mod_spec(mod, layer, tm, tiles_per_row_group, 5),
                  pl.BlockSpec((None, D_MODEL, 2 * D_FF), vec, pipeline_mode=pl.Buffered(1)),
                  pl.BlockSpec((None, D_FF, D_MODEL), vec, pipeline_mode=pl.Buffered(1)),
                  pl.BlockSpec((None, 1, D_MODEL), vec),
                  pl.BlockSpec((None, 1, D_MODEL), vec)],
        out_specs=pl.BlockSpec((tm, D_MODEL), row),
        out_shape=jax.ShapeDtypeStruct(x.shape, F32),
        compiler_params=_params(("arbitrary",)),
        name="ffn",
    )(x, mod, mod, mod, w_gu, w_down, ln_g.reshape(DEPTH, 1, D_MODEL), ln_b.reshape(DEPTH, 1, D_MODEL))


def kernel(x_prompt, x_sample, cache_k, cache_v, state_pool, page_table, c_prompt, c_sample, w_in, lam, subln_g,
           w_pool, pool_scale, w_out, w_ada, b_ada, ln1_g, ln1_b, ln2_g, ln2_b, w_gu, w_down):
    batch, seq, _ = x_prompt.shape
    n_dec, dec_seq, _ = x_sample.shape
    n_pages = page_table.shape[1]
    n_pool_pages = cache_k.shape[1]
    past_len = n_pages * PAGE_SIZE

    w_in_b, w_out_b, w_ada_b, w_pool_b = (w.astype(BF16) for w in (w_in, w_out, w_ada, w_pool))
    w_gu_b, w_down_b = w_gu.astype(BF16), w_down.astype(BF16)

    ada_p, ada_s = _ada_call(c_prompt, jnp.repeat(c_sample, dec_seq, axis=0), w_ada_b, b_ada)
    mod_p = ada_p.reshape(DEPTH * batch, 1, 6 * D_MODEL)
    mod_s = ada_s.reshape(DEPTH * n_dec * dec_seq, 6 * D_MODEL)

    cache_k4 = cache_k.reshape(DEPTH, n_pool_pages, PAGE_SIZE * N_HEADS, V_DIM)
    cache_v4 = cache_v.reshape(DEPTH, n_pool_pages, PAGE_SIZE * N_HEADS, V_DIM)

    rel, diag = _prompt_bias_tables(min(ATTN_TILE, seq))
    slopes = jnp.asarray(_alibi_slopes_log2(), F32)
    sample_tables = _sample_bias_tables(dec_seq)

    p_in, p_mg, p_ff = min(512, seq), min(512, seq), min(512, seq)
    m_s = n_dec * dec_seq
    s_in, s_ff = min(256, m_s), min(256, m_s)
    s_mg_seqs = min(32, n_dec)
    halo_blocks = p_mg // HALO
    mg_tiles = seq // p_mg

    xp = x_prompt.reshape(batch * seq, D_MODEL)
    xs = x_sample.reshape(m_s, D_MODEL)
    bp, bs = [], []
    kv_p = kv_s = None
    for layer in range(DEPTH):
        lam_init = 0.8 - 0.6 * math.exp(-0.3 * layer)

        q, kb, vb, u, gates, kv_p = _inproj_call(xp, mod_p, w_in_b, layer, p_in, seq // p_in, BF16, kv_p)
        o = _attn_prompt_call(q, kb, vb, rel, diag, slopes, lam, subln_g, layer, lam_init, batch, seq)
        u3 = u.reshape(batch, seq, D_POOL)
        halo_spec = pl.BlockSpec(
            (1, HALO, D_POOL),
            lambda i: (i // mg_tiles, jnp.maximum((i % mg_tiles) * halo_blocks - 1, 0), 0))
        xp = _merge_call(xp, o, u3, u3, halo_spec, gates, mod_p, w_pool_b, pool_scale, w_out_b, ln1_g, ln1_b,
                         layer, 1, mg_tiles, 0, mg_tiles)
        xp = _ffn_call(xp, mod_p, w_gu_b, w_down_b, ln2_g, ln2_b, layer, p_ff, seq // p_ff)
        bp.append(u3[:, seq - POOL_BUF:])

        q, kb, vb, u, gates, kv_s = _inproj_call(xs, mod_s, w_in_b, layer, s_in, 1, F32, kv_s)
        o = _attn_sample_call(page_table, q, kb, vb, cache_k4, cache_v4, sample_tables, lam, subln_g,
                              layer, lam_init, dec_seq)
        us = u.reshape(n_dec, dec_seq, D_POOL)
        halo_spec = pl.BlockSpec((None, s_mg_seqs, POOL_BUF, D_POOL), lambda i: (layer, i, 0, 0))
        xs = _merge_call(xs, o, us, state_pool, halo_spec, gates, mod_s, w_pool_b, pool_scale, w_out_b,
                         ln1_g, ln1_b, layer, s_mg_seqs, 1, past_len, None)
        xs = _ffn_call(xs, mod_s, w_gu_b, w_down_b, ln2_g, ln2_b, layer, s_ff, 1)
        bs.append(jnp.concatenate([state_pool[layer], us], axis=1)[:, -POOL_BUF:])

    shape_p = (DEPTH, batch, seq, N_HEADS, V_DIM)
    shape_s = (DEPTH, n_dec, dec_seq, N_HEADS, V_DIM)
    return (xp.reshape(batch, seq, D_MODEL), xs.reshape(n_dec, dec_seq, D_MODEL),
            kv_p[0].reshape(shape_p), kv_p[1].reshape(shape_p), jnp.stack(bp),
            kv_s[0].reshape(shape_s), kv_s[1].reshape(shape_s), jnp.stack(bs))
```
